```python
import math
import jax, jax.numpy as jnp
from jax import lax
import numpy as np

D_MODEL = 4096
BATCH = 1
SEQ = 16384
DEPTH = 2
DEC_BATCH = 16
DEC_SEQ = 16
PAST_LEN = 2048

CHUNK = 64
Q_BLOCK = 128
GROUP_W = D_MODEL // 4
D_MIX = 4 * GROUP_W
CONV_W = 3
MLA_HEADS = 8
MLA_NOPE = 128
MLA_ROPE = 64
MLA_V = 128
MLA_Q_RANK = 1024
MLA_KV_RANK = 512
ROPE_THETA = 10000.0
DIFF_HEADS = 8
DIFF_DIM = 64
DIFF_V = 2 * DIFF_DIM
DIFF_QK = DIFF_HEADS * 2 * DIFF_DIM
MLP_CHUNK = 128
MLP_GROUPS = 8
MLP_GDIM = GROUP_W // MLP_GROUPS
D_FF = 14336
N_EXPERTS = 8
TOP_K = 2
MOE_BLOCK = 128
N_DENSE = (DEPTH + 1) // 2
N_MOE = DEPTH // 2
DN_ALPHA = (2 * DEPTH) ** 0.25
DN_BETA = (8 * DEPTH) ** -0.25
NEG_INF = -1e30
IN_COLS = 3 * GROUP_W + (MLA_Q_RANK + MLA_KV_RANK + MLA_ROPE) + (2 * DIFF_QK + DIFF_HEADS * DIFF_V) + 2 * GROUP_W

kernel_name = 'hybrid_stream_encoder_step'


def _layernorm(x, g, b, eps=1e-5):
    xf = x.astype(jnp.float32)
    mu = jnp.mean(xf, -1, keepdims=True)
    var = jnp.mean(jnp.square(xf - mu), -1, keepdims=True)
    return ((xf - mu) * lax.rsqrt(var + eps) * g.astype(jnp.float32) + b.astype(jnp.float32)).astype(x.dtype)


def _rmsnorm(x, g, eps=1e-6):
    xf = x.astype(jnp.float32)
    return (xf * lax.rsqrt(jnp.mean(xf * xf, -1, keepdims=True) + eps) * g.astype(jnp.float32)).astype(x.dtype)


def _rope(x, pos):
    half = x.shape[-1] // 2
    freqs = ROPE_THETA ** (-jnp.arange(half, dtype=jnp.float32) / half)
    ang = pos.astype(jnp.float32)[:, None] * freqs
    if x.ndim == 4:
        ang = ang[:, None, :]
    cos, sin = jnp.cos(ang), jnp.sin(ang)
    xf = x.astype(jnp.float32)
    x1, x2 = xf[..., :half], xf[..., half:]
    return jnp.concatenate([x1 * cos - x2 * sin, x2 * cos + x1 * sin], -1).astype(x.dtype)


def _chunk_allowed(q_pos, k_pos):
    return (k_pos[None, :] // CHUNK) <= (q_pos[:, None] // CHUNK)


def _over_query_blocks(fn, q, q_pos):
    T = q.shape[1]
    if T <= Q_BLOCK or T % Q_BLOCK:
        return fn(q, q_pos)
    nb = T // Q_BLOCK
    qb = jnp.moveaxis(q.reshape(q.shape[0], nb, Q_BLOCK, *q.shape[2:]), 1, 0)
    out = lax.map(lambda a: fn(a[0], a[1]), (qb, q_pos.reshape(nb, Q_BLOCK)))
    out = jnp.moveaxis(out, 0, 1)
    return out.reshape(out.shape[0], T, *out.shape[3:])


def _split_in_proj(h):
    sizes = [GROUP_W, GROUP_W, GROUP_W,
             MLA_Q_RANK, MLA_KV_RANK, MLA_ROPE,
             DIFF_QK, DIFF_QK, DIFF_HEADS * DIFF_V,
             GROUP_W, GROUP_W]
    idx = np.cumsum(sizes)[:-1].tolist()
    return jnp.split(h, idx, axis=-1)


def _short_conv(xb, xc, xh, past, conv_w):
    z = xc * xh
    B, T, C = z.shape
    prev = jnp.zeros((B, CONV_W - 1, C), z.dtype) if past is None else past.astype(z.dtype)
    zp = jnp.concatenate([prev, z], 1)
    y = sum(conv_w[k] * zp[:, k:k + T] for k in range(CONV_W))
    return xb * y, zp[:, -(CONV_W - 1):]


def _mla(x_cq, x_ckv, x_kr, pos, past_ckv, past_kr, q_norm, w_qb, kv_norm, w_kvb):
    B, T, _ = x_cq.shape
    c_q = _rmsnorm(x_cq, q_norm)
    q = (c_q @ w_qb).reshape(B, T, MLA_HEADS, MLA_NOPE + MLA_ROPE)
    q = jnp.concatenate([q[..., :MLA_NOPE], _rope(q[..., MLA_NOPE:], pos)], -1)
    c_kv = _rmsnorm(x_ckv, kv_norm)
    k_r = _rope(x_kr, pos)
    if past_ckv is None:
        all_ckv, all_kr, k_pos = c_kv, k_r, pos
    else:
        all_ckv = jnp.concatenate([past_ckv.astype(c_kv.dtype), c_kv], 1)
        all_kr = jnp.concatenate([past_kr.astype(k_r.dtype), k_r], 1)
        k_pos = jnp.arange(all_ckv.shape[1], dtype=jnp.int32)
    Tk = all_ckv.shape[1]
    kv = (all_ckv @ w_kvb).reshape(B, Tk, MLA_HEADS, MLA_NOPE + MLA_V)
    k = jnp.concatenate([kv[..., :MLA_NOPE],
                         jnp.broadcast_to(all_kr[:, :, None, :], (B, Tk, MLA_HEADS, MLA_ROPE))], -1)
    v = kv[..., MLA_NOPE:]
    scale = (MLA_NOPE + MLA_ROPE) ** -0.5

    def core(qb, qp):
        s = jnp.einsum('bqhd,bkhd->bhqk', qb, k).astype(jnp.float32) * scale
        s = jnp.where(_chunk_allowed(qp, k_pos), s, NEG_INF)
        p = jax.nn.softmax(s, axis=-1).astype(v.dtype)
        return jnp.einsum('bhqk,bkhd->bqhd', p, v)

    o = _over_query_blocks(core, q, pos)
    return o.reshape(B, T, MLA_HEADS * MLA_V), c_kv, k_r


def _diff(xq, xk, xv, pos, past_k, past_v, lq1, lk1, lq2, lk2, subln_g, lambda_init):
    B, T, _ = xq.shape
    q = xq.reshape(B, T, DIFF_HEADS, 2 * DIFF_DIM)
    new_k = xk.reshape(B, T, DIFF_HEADS, 2 * DIFF_DIM)
    new_v = xv.reshape(B, T, DIFF_HEADS, DIFF_V)
    if past_k is None:
        k, v, k_pos = new_k, new_v, pos
    else:
        k = jnp.concatenate([past_k.astype(new_k.dtype), new_k], 1)
        v = jnp.concatenate([past_v.astype(new_v.dtype), new_v], 1)
        k_pos = jnp.arange(k.shape[1], dtype=jnp.int32)
    f32 = jnp.float32
    lam = (jnp.exp(jnp.sum(lq1.astype(f32) * lk1.astype(f32)))
           - jnp.exp(jnp.sum(lq2.astype(f32) * lk2.astype(f32))) + lambda_init)
    slopes = 2.0 ** (-8.0 * jnp.arange(1, DIFF_HEADS + 1, dtype=f32) / DIFF_HEADS)
    scale = DIFF_DIM ** -0.5
    k1, k2 = k[..., :DIFF_DIM], k[..., DIFF_DIM:]

    def core(qb, qp):
        q1, q2 = qb[..., :DIFF_DIM], qb[..., DIFF_DIM:]
        dist = jnp.abs(qp[:, None] - k_pos[None, :]).astype(f32)
        bias = jnp.where(_chunk_allowed(qp, k_pos)[None], -slopes[:, None, None] * dist[None], NEG_INF)
        s1 = jnp.einsum('bqhd,bkhd->bhqk', q1, k1).astype(f32) * scale + bias
        s2 = jnp.einsum('bqhd,bkhd->bhqk', q2, k2).astype(f32) * scale + bias
        p = jax.nn.softmax(s1, axis=-1) - lam * jax.nn.softmax(s2, axis=-1)
        return jnp.einsum('bhqk,bkhd->bqhd', p.astype(v.dtype), v)

    o = _over_query_blocks(core, q, pos)
    o = _rmsnorm(o, subln_g, eps=1e-5) * (1.0 - lambda_init)
    return o.reshape(B, T, DIFF_HEADS * DIFF_V), new_k, new_v


def _chunk_mlp(xu, xv, norm_g, norm_b, ws, bs):
    u = jax.nn.gelu(xu)
    v = _layernorm(jax.nn.gelu(xv), norm_g, norm_b)
    B, T, _ = v.shape
    L = MLP_CHUNK if T % MLP_CHUNK == 0 else T
    w = ws[:, :L, :L] * jnp.tril(jnp.ones((L, L), ws.dtype))
    vb = v.reshape(B, T // L, L, MLP_GROUPS, MLP_GDIM)
    s = jnp.einsum('gij,bnjgc->bnigc', w, vb) + jnp.transpose(bs[:, :L])[None, None, :, :, None]
    return u * s.reshape(B, T, GROUP_W), v


def _swiglu(x, w1, w3, w2):
    return (jax.nn.silu(x @ w1) * (x @ w3)) @ w2


def _moe(x, w_router, b_router, w1, w3, w2):
    shp = x.shape
    xt = x.reshape(-1, shp[-1])
    T = xt.shape[0]
    logits = (xt @ w_router).astype(jnp.float32) + b_router.astype(jnp.float32)
    top_v, top_i = lax.top_k(logits, TOP_K)
    gates = jax.nn.softmax(top_v, axis=-1)
    A = T * TOP_K
    flat_e = top_i.reshape(-1).astype(jnp.int32)
    flat_t = jnp.repeat(jnp.arange(T, dtype=jnp.int32), TOP_K)
    flat_g = gates.reshape(-1)
    order = jnp.argsort(flat_e)
    se, st, sg = flat_e[order], flat_t[order], flat_g[order]
    counts = jnp.bincount(flat_e, length=N_EXPERTS).astype(jnp.int32)
    starts = jnp.cumsum(counts) - counts
    padded = (counts + MOE_BLOCK - 1) // MOE_BLOCK * MOE_BLOCK
    pend = jnp.cumsum(padded)
    pstart = pend - padded
    dest = pstart[se] + jnp.arange(A, dtype=jnp.int32) - starts[se]
    nb = -(-A // MOE_BLOCK) + N_EXPERTS
    tok = jnp.zeros(nb * MOE_BLOCK, jnp.int32).at[dest].set(st)
    gate = jnp.zeros(nb * MOE_BLOCK, jnp.float32).at[dest].set(sg)
    blk_e = jnp.minimum(jnp.searchsorted(pend, jnp.arange(nb, dtype=jnp.int32) * MOE_BLOCK, side='right'),
                        N_EXPERTS - 1)

    def run(args):
        e, t, g = args
        xb = xt[t]
        h = jax.nn.silu(xb @ w1[e]) * (xb @ w3[e])
        return (h @ w2[e]) * g[:, None].astype(xt.dtype)

    yb = lax.map(run, (blk_e, tok.reshape(nb, MOE_BLOCK), gate.reshape(nb, MOE_BLOCK)))
    y = jnp.zeros_like(xt).at[tok].add(yb.reshape(-1, shp[-1]))
    return y.reshape(shp)


def _layer(x, pos, l, past, p):
    (a_b, a_c, a_h, b_cq, b_ckv, b_kr, c_q, c_k, c_v, d_u, d_v) = _split_in_proj(x @ p['w_in'])
    if past is None:
        past_ckv = past_kr = past_dk = past_dv = past_conv = None
    else:
        past_ckv, past_kr, past_dk, past_dv, past_conv = past
    y_a, conv_state = _short_conv(a_b, a_c, a_h, past_conv, p['conv_w'])
    y_b, ckv, kr = _mla(b_cq, b_ckv, b_kr, pos, past_ckv, past_kr,
                        p['mla_q_norm'], p['mla_w_qb'], p['mla_kv_norm'], p['mla_w_kvb'])
    lambda_init = 0.8 - 0.6 * math.exp(-0.3 * l)
    y_c, dk, dv = _diff(c_q, c_k, c_v, pos, past_dk, past_dv,
                        p['diff_lq1'], p['diff_lk1'], p['diff_lq2'], p['diff_lk2'], p['diff_subln'], lambda_init)
    y_d, mv = _chunk_mlp(d_u, d_v, p['mlp_v_norm_g'], p['mlp_v_norm_b'], p['mlp_ws'], p['mlp_bs'])
    mix = jnp.concatenate([y_a, y_b, y_c, y_d], -1) @ p['w_out']
    x = _layernorm(DN_ALPHA * x + mix, p['ln1_g'], p['ln1_b'])
    if l % 2 == 0:
        f = _swiglu(x, p['ffn_w1'], p['ffn_w3'], p['ffn_w2'])
    else:
        f = _moe(x, p['moe_router'], p['moe_router_b'], p['moe_w1'], p['moe_w3'], p['moe_w2'])
    x = _layernorm(DN_ALPHA * x + f, p['ln2_g'], p['ln2_b'])
    return x, (ckv, kr, dk, dv, conv_state, mv)


def setup_inputs(seed: int = 0) -> dict:
    key = jax.random.key(seed)
    ks = iter(jax.random.split(key, 40))

    def nrm(shape, scale):
        return jax.random.normal(next(ks), shape, jnp.float32) * scale

    def gain(shape):
        return 1.0 + nrm(shape, 0.02)

    return {
        'x_prompt': nrm((BATCH, SEQ, D_MODEL), 1.0),
        'x_sample': nrm((DEC_BATCH, DEC_SEQ, D_MODEL), 1.0),
        'cache_mla_ckv': nrm((DEPTH, DEC_BATCH, PAST_LEN, MLA_KV_RANK), 1.0),
        'cache_mla_krope': nrm((DEPTH, DEC_BATCH, PAST_LEN, MLA_ROPE), 1.0),
        'cache_diff_k': nrm((DEPTH, DEC_BATCH, PAST_LEN, DIFF_HEADS, 2 * DIFF_DIM), 1.0),
        'cache_diff_v': nrm((DEPTH, DEC_BATCH, PAST_LEN, DIFF_HEADS, DIFF_V), 1.0),
        'state_conv': nrm((DEPTH, DEC_BATCH, CONV_W - 1, GROUP_W), 1.0),
        'w_in': nrm((DEPTH, D_MODEL, IN_COLS), D_MODEL ** -0.5),
        'conv_w': nrm((DEPTH, CONV_W, GROUP_W), CONV_W ** -0.5),
        'mla_q_norm': gain((DEPTH, MLA_Q_RANK)),
        'mla_w_qb': nrm((DEPTH, MLA_Q_RANK, MLA_HEADS * (MLA_NOPE + MLA_ROPE)), MLA_Q_RANK ** -0.5),
        'mla_kv_norm': gain((DEPTH, MLA_KV_RANK)),
        'mla_w_kvb': nrm((DEPTH, MLA_KV_RANK, MLA_HEADS * (MLA_NOPE + MLA_V)), MLA_KV_RANK ** -0.5),
        'diff_lq1': nrm((DEPTH, DIFF_DIM), 0.1),
        'diff_lk1': nrm((DEPTH, DIFF_DIM), 0.1),
        'diff_lq2': nrm((DEPTH, DIFF_DIM), 0.1),
        'diff_lk2': nrm((DEPTH, DIFF_DIM), 0.1),
        'diff_subln': gain((DEPTH, DIFF_V)),
        'mlp_v_norm_g': gain((DEPTH, GROUP_W)),
        'mlp_v_norm_b': nrm((DEPTH, GROUP_W), 0.02),
        'mlp_ws': nrm((DEPTH, MLP_GROUPS, MLP_CHUNK, MLP_CHUNK), MLP_CHUNK ** -0.5),
        'mlp_bs': gain((DEPTH, MLP_GROUPS, MLP_CHUNK)),
        'w_out': nrm((DEPTH, D_MIX, D_MODEL), D_MIX ** -0.5 * DN_BETA),
        'ln1_g': gain((DEPTH, D_MODEL)),
        'ln1_b': nrm((DEPTH, D_MODEL), 0.02),
        'ln2_g': gain((DEPTH, D_MODEL)),
        'ln2_b': nrm((DEPTH, D_MODEL), 0.02),
        'ffn_w1': nrm((N_DENSE, D_MODEL, D_FF), D_MODEL ** -0.5),
        'ffn_w3': nrm((N_DENSE, D_MODEL, D_FF), D_MODEL ** -0.5),
        'ffn_w2': nrm((N_DENSE, D_FF, D_MODEL), D_FF ** -0.5 * DN_BETA),
        'moe_router': nrm((N_MOE, D_MODEL, N_EXPERTS), D_MODEL ** -0.5),
        'moe_router_b': nrm((N_MOE, N_EXPERTS), 0.01),
        'moe_w1': nrm((N_MOE, N_EXPERTS, D_MODEL, D_FF), D_MODEL ** -0.5),
        'moe_w3': nrm((N_MOE, N_EXPERTS, D_MODEL, D_FF), D_MODEL ** -0.5),
        'moe_w2': nrm((N_MOE, N_EXPERTS, D_FF, D_MODEL), D_FF ** -0.5 * DN_BETA),
    }


def reference(x_prompt, x_sample, cache_mla_ckv, cache_mla_krope, cache_diff_k, cache_diff_v, state_conv,
              w_in, conv_w, mla_q_norm, mla_w_qb, mla_kv_norm, mla_w_kvb,
              diff_lq1, diff_lk1, diff_lq2, diff_lk2, diff_subln,
              mlp_v_norm_g, mlp_v_norm_b, mlp_ws, mlp_bs, w_out,
              ln1_g, ln1_b, ln2_g, ln2_b, ffn_w1, ffn_w3, ffn_w2,
              moe_router, moe_router_b, moe_w1, moe_w3, moe_w2):
    pos_p = jnp.arange(x_prompt.shape[1], dtype=jnp.int32)
    past_len = cache_mla_ckv.shape[2]
    pos_s = past_len + jnp.arange(x_sample.shape[1], dtype=jnp.int32)
    xp, xs = x_prompt, x_sample
    st_p, st_s = [], []
    for l in range(DEPTH):
        p = {'w_in': w_in[l], 'conv_w': conv_w[l],
             'mla_q_norm': mla_q_norm[l], 'mla_w_qb': mla_w_qb[l],
             'mla_kv_norm': mla_kv_norm[l], 'mla_w_kvb': mla_w_kvb[l],
             'diff_lq1': diff_lq1[l], 'diff_lk1': diff_lk1[l], 'diff_lq2': diff_lq2[l], 'diff_lk2': diff_lk2[l],
             'diff_subln': diff_subln[l],
             'mlp_v_norm_g': mlp_v_norm_g[l], 'mlp_v_norm_b': mlp_v_norm_b[l],
             'mlp_ws': mlp_ws[l], 'mlp_bs': mlp_bs[l], 'w_out': w_out[l],
             'ln1_g': ln1_g[l], 'ln1_b': ln1_b[l], 'ln2_g': ln2_g[l], 'ln2_b': ln2_b[l]}
        if l % 2 == 0:
            j = l // 2
            p.update(ffn_w1=ffn_w1[j], ffn_w3=ffn_w3[j], ffn_w2=ffn_w2[j])
        else:
            j = l // 2
            p.update(moe_router=moe_router[j], moe_router_b=moe_router_b[j],
                     moe_w1=moe_w1[j], moe_w3=moe_w3[j], moe_w2=moe_w2[j])
        xp, sp = _layer(xp, pos_p, l, None, p)
        xs, ss = _layer(xs, pos_s, l, (cache_mla_ckv[l], cache_mla_krope[l], cache_diff_k[l],
                                       cache_diff_v[l], state_conv[l]), p)
        st_p.append(sp)
        st_s.append(ss)
    new_ckv_p = jnp.stack([s[0] for s in st_p])
    new_kr_p = jnp.stack([s[1] for s in st_p])
    new_dk_p = jnp.stack([s[2] for s in st_p])
    new_dv_p = jnp.stack([s[3] for s in st_p])
    new_conv_p = jnp.stack([s[4] for s in st_p])
    new_ckv_s = jnp.stack([s[0] for s in st_s])
    new_kr_s = jnp.stack([s[1] for s in st_s])
    new_dk_s = jnp.stack([s[2] for s in st_s])
    new_dv_s = jnp.stack([s[3] for s in st_s])
    new_conv_s = jnp.stack([s[4] for s in st_s])
    new_mlpv_s = jnp.stack([s[5] for s in st_s])
    return (xp, xs, new_ckv_p, new_kr_p, new_dk_p, new_dv_p, new_conv_p,
            new_ckv_s, new_kr_s, new_dk_s, new_dv_s, new_conv_s, new_mlpv_s)
```

```python
import functools
import math

import numpy as np
import jax
import jax.numpy as jnp
from jax import lax
from jax.experimental import pallas as pl
from jax.experimental.pallas import tpu as pltpu

F32 = jnp.float32
BF16 = jnp.bfloat16

CHUNK_SHIFT = 6
GROUP_W = 1024
CONV_W = 3
HEADS = 8
MLA_NOPE = 128
MLA_ROPE = 64
MLA_KV_RANK = 512
HEAD_W = 128
DIFF_DIM = 64
MLP_CHUNK = 128
TOP_K = 2
ROPE_THETA = 10000.0
NEG_INF = -1e30
LANES = 128
VMEM_LIMIT = 56 * 1024 * 1024

C_AB, C_AC, C_AH, C_CQ = 0, 1024, 2048, 3072
C_DQ, C_DK, C_DV = 4096, 5120, 6144
C_DU, C_DVV = 7168, 8192
C_CKV = 9216
C_KR = 9728


def _params(sem, vmem=VMEM_LIMIT):
    return pltpu.CompilerParams(dimension_semantics=sem, vmem_limit_bytes=vmem)


def _pick(n, target, mult):
    best = None
    for d in range(mult, min(n, target) + 1, mult):
        if n % d == 0:
            best = d
    assert best is not None, (n, target, mult)
    return best


def _mm_kernel(x_ref, w_ref, o_ref):
    o_ref[...] = jnp.dot(x_ref[...], w_ref[...], preferred_element_type=F32).astype(o_ref.dtype)


def _mm(x, w, bm, bn, out_dtype, name):
    m, k = x.shape
    n = w.shape[1]
    return pl.pallas_call(
        _mm_kernel,
        grid=(m // bm, n // bn),
        in_specs=[pl.BlockSpec((bm, k), lambda i, j: (i, 0)),
                  pl.BlockSpec((k, bn), lambda i, j: (0, j))],
        out_specs=pl.BlockSpec((bm, bn), lambda i, j: (i, j)),
        out_shape=jax.ShapeDtypeStruct((m, n), out_dtype),
        compiler_params=_params(("parallel", "arbitrary")),
        name=name,
    )(x, w)


def _conv_prompt_kernel(b_ref, c_ref, h_ref, cp_ref, hp_ref, w_ref, y_ref, st_ref, zz_ref):
    i = pl.program_id(0)
    bm = b_ref.shape[0]
    z = c_ref[...] * h_ref[...]
    zprev = cp_ref[...] * hp_ref[...]
    zz_ref[0:8, :] = jnp.where(i == 0, 0.0, zprev)
    zz_ref[8:, :] = z
    w = w_ref[...]
    y = w[0:1] * zz_ref[pl.ds(6, bm), :] + w[1:2] * zz_ref[pl.ds(7, bm), :] + w[2:3] * z
    y_ref[...] = (b_ref[...] * y).astype(y_ref.dtype)
    st_ref[...] = z[bm - 8:, :]


def _conv_prompt(h, conv_w, tp, bm):
    nb = tp // bm
    r8 = bm // 8
    cw = GROUP_W
    y, st = pl.pallas_call(
        _conv_prompt_kernel,
        grid=(nb,),
        in_specs=[pl.BlockSpec((bm, cw), lambda i: (i, C_AB // cw)),
                  pl.BlockSpec((bm, cw), lambda i: (i, C_AC // cw)),
                  pl.BlockSpec((bm, cw), lambda i: (i, C_AH // cw)),
                  pl.BlockSpec((8, cw), lambda i: (jnp.maximum(i * r8 - 1, 0), C_AC // cw)),
                  pl.BlockSpec((8, cw), lambda i: (jnp.maximum(i * r8 - 1, 0), C_AH // cw)),
                  pl.BlockSpec((CONV_W, cw), lambda i: (0, 0))],
        out_specs=[pl.BlockSpec((bm, cw), lambda i: (i, 0)),
                   pl.BlockSpec((8, cw), lambda i: (0, 0))],
        out_shape=[jax.ShapeDtypeStruct((tp, cw), BF16),
                   jax.ShapeDtypeStruct((8, cw), F32)],
        scratch_shapes=[pltpu.VMEM((bm + 8, cw), F32)],
        compiler_params=_params(("arbitrary",)),
        name="conv_prompt",
    )(h, h, h, h, h, conv_w)
    return y, st[8 - (CONV_W - 1):]


def _conv_sample_kernel(b_ref, c_ref, h_ref, past_ref, w_ref, y_ref, st_ref, zz_ref):
    ts = b_ref.shape[0]
    z = c_ref[...] * h_ref[...]
    zz_ref[pl.ds(6, 2), :] = past_ref[0]
    zz_ref[pl.ds(8, ts), :] = z
    w = w_ref[...]
    y = w[0:1] * zz_ref[pl.ds(6, ts), :] + w[1:2] * zz_ref[pl.ds(7, ts), :] + w[2:3] * z
    y_ref[...] = (b_ref[...] * y).astype(y_ref.dtype)
    st_ref[0] = zz_ref[pl.ds(ts + 6, 2), :]


def _conv_sample(h, past, conv_w, tp, nb, ts):
    cw = GROUP_W
    r0 = tp // ts
    return pl.pallas_call(
        _conv_sample_kernel,
        grid=(nb,),
        in_specs=[pl.BlockSpec((ts, cw), lambda b: (r0 + b, C_AB // cw)),
                  pl.BlockSpec((ts, cw), lambda b: (r0 + b, C_AC // cw)),
                  pl.BlockSpec((ts, cw), lambda b: (r0 + b, C_AH // cw)),
                  pl.BlockSpec((1, CONV_W - 1, cw), lambda b: (b, 0, 0)),
                  pl.BlockSpec((CONV_W, cw), lambda b: (0, 0))],
        out_specs=[pl.BlockSpec((ts, cw), lambda b: (b, 0)),
                   pl.BlockSpec((1, CONV_W - 1, cw), lambda b: (b, 0, 0))],
        out_shape=[jax.ShapeDtypeStruct((nb * ts, cw), BF16),
                   jax.ShapeDtypeStruct((nb, CONV_W - 1, cw), F32)],
        scratch_shapes=[pltpu.VMEM((ts + 8, cw), F32)],
        compiler_params=_params(("parallel",)),
        name="conv_sample",
    )(h, h, h, past, conv_w)


def _rope_spread(r, cos, sin):
    return r * cos + pltpu.roll(r, 64, 1) * sin


def _mla_q_kernel(x_ref, g_ref, w_ref, cos_ref, sin_ref, o_ref, *, scale):
    x = x_ref[...]
    cn = x * lax.rsqrt(jnp.mean(x * x, axis=-1, keepdims=True) + 1e-6) * g_ref[...]
    q = jnp.dot(cn.astype(BF16), w_ref[...], preferred_element_type=F32)
    cos = cos_ref[...]
    sin = sin_ref[...]
    for hd in range(HEADS):
        c0 = hd * 2 * LANES
        o_ref[:, c0:c0 + LANES] = (q[:, c0:c0 + LANES] * scale).astype(o_ref.dtype)
        r = _rope_spread(q[:, c0 + LANES:c0 + 2 * LANES], cos, sin)
        o_ref[:, c0 + LANES:c0 + 2 * LANES] = (r * scale).astype(o_ref.dtype)


def _mla_q(h, g, w, cos, sin, bm):
    t = h.shape[0]
    n = w.shape[1]
    scale = (MLA_NOPE + MLA_ROPE) ** -0.5
    return pl.pallas_call(
        functools.partial(_mla_q_kernel, scale=scale),
        grid=(t // bm,),
        in_specs=[pl.BlockSpec((bm, GROUP_W), lambda i: (i, C_CQ // GROUP_W)),
                  pl.BlockSpec((1, GROUP_W), lambda i: (0, 0)),
                  pl.BlockSpec(w.shape, lambda i: (0, 0)),
                  pl.BlockSpec((bm, LANES), lambda i: (i, 0)),
                  pl.BlockSpec((bm, LANES), lambda i: (i, 0))],
        out_specs=pl.BlockSpec((bm, n), lambda i: (i, 0)),
        out_shape=jax.ShapeDtypeStruct((t, n), BF16),
        compiler_params=_params(("parallel",)),
        name="mla_q",
    )(h, g, w, cos, sin)


def _mla_kv_kernel(x_ref, kr_ref, g_ref, w_ref, cos_ref, sin_ref, ckv_ref, krf_ref, kv_ref, krb_ref):
    x = x_ref[...]
    cn = x * lax.rsqrt(jnp.mean(x * x, axis=-1, keepdims=True) + 1e-6) * g_ref[...]
    ckv_ref[...] = cn
    kv_ref[...] = jnp.dot(cn.astype(BF16), w_ref[...], preferred_element_type=F32).astype(kv_ref.dtype)
    r = _rope_spread(kr_ref[...], cos_ref[...], sin_ref[...])
    krf_ref[...] = r
    krb_ref[...] = r.astype(krb_ref.dtype)


def _mla_kv(h, g, w, cos, sin, bm):
    t = h.shape[0]
    n = w.shape[1]
    return pl.pallas_call(
        _mla_kv_kernel,
        grid=(t // bm,),
        in_specs=[pl.BlockSpec((bm, MLA_KV_RANK), lambda i: (i, C_CKV // MLA_KV_RANK)),
                  pl.BlockSpec((bm, LANES), lambda i: (i, C_KR // LANES)),
                  pl.BlockSpec((1, MLA_KV_RANK), lambda i: (0, 0)),
                  pl.BlockSpec(w.shape, lambda i: (0, 0)),
                  pl.BlockSpec((bm, LANES), lambda i: (i, 0)),
                  pl.BlockSpec((bm, LANES), lambda i: (i, 0))],
        out_specs=[pl.BlockSpec((bm, MLA_KV_RANK), lambda i: (i, 0)),
                   pl.BlockSpec((bm, LANES), lambda i: (i, 0)),
                   pl.BlockSpec((bm, n), lambda i: (i, 0)),
                   pl.BlockSpec((bm, LANES), lambda i: (i, 0))],
        out_shape=[jax.ShapeDtypeStruct((t, MLA_KV_RANK), F32),
                   jax.ShapeDtypeStruct((t, LANES), F32),
                   jax.ShapeDtypeStruct((t, n), BF16),
                   jax.ShapeDtypeStruct((t, LANES), BF16)],
        compiler_params=_params(("parallel",)),
        name="mla_kv",
    )(h, h, g, w, cos, sin)


def _nt_dot(a, b):
    return lax.dot_general(a, b, (((1,), (1,)), ((), ())), preferred_element_type=F32)


def _chunk_mask(qpos, kpos):
    return (kpos >> CHUNK_SHIFT) <= (qpos >> CHUNK_SHIFT)


def _tri_pairs(tp, bq, bk):
    qi, kj, last = [], [], []
    for i in range(tp // bq):
        q_hi = (i + 1) * bq - 1
        k_end = ((q_hi >> CHUNK_SHIFT) + 1) << CHUNK_SHIFT
        nj = -(-min(k_end, tp) // bk)
        for j in range(nj):
            qi.append(i)
            kj.append(j)
            last.append(1 if j == nj - 1 else 0)
    return (jnp.asarray(qi, jnp.int32), jnp.asarray(kj, jnp.int32), jnp.asarray(last, jnp.int32))


def _flash_mla_kernel(qi_ref, kj_ref, last_ref, q_ref, kn_ref, kr_ref, v_ref, o_ref, m_ref, l_ref, acc_ref):
    p = pl.program_id(1)
    bq = q_ref.shape[0]
    bk = kn_ref.shape[0]
    j = kj_ref[p]

    @pl.when(j == 0)
    def _():
        m_ref[...] = jnp.full(m_ref.shape, NEG_INF, F32)
        l_ref[...] = jnp.zeros(l_ref.shape, F32)
        acc_ref[...] = jnp.zeros(acc_ref.shape, F32)

    k = jnp.concatenate([kn_ref[...], kr_ref[...]], axis=1)
    s = _nt_dot(q_ref[...], k)
    qpos = qi_ref[p] * bq + lax.broadcasted_iota(jnp.int32, (bq, 1), 0)
    kpos = j * bk + lax.broadcasted_iota(jnp.int32, (1, bk), 1)
    s = jnp.where(_chunk_mask(qpos, kpos), s, NEG_INF)
    m_old = m_ref[...]
    m_new = jnp.maximum(m_old, jnp.max(s, axis=1, keepdims=True))
    alpha = jnp.exp(m_old - m_new)
    e = jnp.exp(s - m_new)
    l_ref[...] = alpha * l_ref[...] + jnp.sum(e, axis=1, keepdims=True)
    acc_ref[...] = alpha * acc_ref[...] + jnp.dot(e.astype(BF16), v_ref[...], preferred_element_type=F32)
    m_ref[...] = m_new

    @pl.when(last_ref[p] == 1)
    def _():
        o_ref[...] = (acc_ref[...] / l_ref[...]).astype(o_ref.dtype)


def _flash_mla(q, kv, krb, tp, bq, bk):
    qi, kj, last = _tri_pairs(tp, bq, bk)
    grid_spec = pltpu.PrefetchScalarGridSpec(
        num_scalar_prefetch=3,
        grid=(HEADS, qi.shape[0]),
        in_specs=[pl.BlockSpec((bq, 2 * LANES), lambda h, p, qi, kj, la: (qi[p], h)),
                  pl.BlockSpec((bk, LANES), lambda h, p, qi, kj, la: (kj[p], h)),
                  pl.BlockSpec((bk, LANES), lambda h, p, qi, kj, la: (kj[p], 0)),
                  pl.BlockSpec((bk, LANES), lambda h, p, qi, kj, la: (kj[p], HEADS + h))],
        out_specs=pl.BlockSpec((bq, LANES), lambda h, p, qi, kj, la: (qi[p], h)),
        scratch_shapes=[pltpu.VMEM((bq, 1), F32), pltpu.VMEM((bq, 1), F32), pltpu.VMEM((bq, LANES), F32)],
    )
    return pl.pallas_call(
        _flash_mla_kernel,
        grid_spec=grid_spec,
        out_shape=jax.ShapeDtypeStruct((tp, HEADS * LANES), BF16),
        compiler_params=_params(("parallel", "arbitrary")),
        name="flash_mla",
    )(qi, kj, last, q, kv, krb, kv)


def _samp_mla_kernel(q_ref, kn_ref, kr_ref, v_ref, o_ref, *, past):
    ts = q_ref.shape[0]
    tk = kn_ref.shape[1]
    k = jnp.concatenate([kn_ref[0], kr_ref[0]], axis=1)
    s = _nt_dot(q_ref[...], k)
    qpos = past + lax.broadcasted_iota(jnp.int32, (ts, 1), 0)
    kpos = lax.broadcasted_iota(jnp.int32, (1, tk), 1)
    s = jnp.where(_chunk_mask(qpos, kpos) & (kpos < past + ts), s, NEG_INF)
    e = jnp.exp(s - jnp.max(s, axis=1, keepdims=True))
    pv = jnp.dot(e.astype(BF16), v_ref[0], preferred_element_type=F32)
    o_ref[...] = (pv / jnp.sum(e, axis=1, keepdims=True)).astype(o_ref.dtype)


def _samp_mla(q, kvfull, krfull, tp, nb, ts, past):
    r0 = tp // ts
    tk = kvfull.shape[1]
    return pl.pallas_call(
        functools.partial(_samp_mla_kernel, past=past),
        grid=(nb, HEADS),
        in_specs=[pl.BlockSpec((ts, 2 * LANES), lambda b, h: (r0 + b, h)),
                  pl.BlockSpec((1, tk, LANES), lambda b, h: (b, 0, h)),
                  pl.BlockSpec((1, tk, LANES), lambda b, h: (b, 0, 0)),
                  pl.BlockSpec((1, tk, LANES), lambda b, h: (b, 0, HEADS + h))],
        out_specs=pl.BlockSpec((ts, LANES), lambda b, h: (b, h)),
        out_shape=jax.ShapeDtypeStruct((nb * ts, HEADS * LANES), BF16),
        compiler_params=_params(("parallel", "parallel")),
        name="samp_mla",
    )(q, kvfull, krfull, kvfull)


def _split_q(q, scale):
    lane = lax.broadcasted_iota(jnp.int32, q.shape, 1)
    qs = q * scale
    qa = jnp.where(lane < DIFF_DIM, qs, 0.0).astype(BF16)
    qb = jnp.where(lane >= DIFF_DIM, qs, 0.0).astype(BF16)
    return qa, qb


def _subln(o, g, lambda_init):
    o = o * lax.rsqrt(jnp.mean(o * o, axis=-1, keepdims=True) + 1e-5) * g
    return o * (1.0 - lambda_init)


def _flash_diff_kernel(qi_ref, kj_ref, last_ref, slope_ref, lam_ref, q_ref, k_ref, v_ref, g_ref, o_ref,
                       m1_ref, l1_ref, a1_ref, m2_ref, l2_ref, a2_ref, *, lambda_init):
    hd = pl.program_id(0)
    p = pl.program_id(1)
    bq = q_ref.shape[0]
    bk = k_ref.shape[0]
    j = kj_ref[p]

    @pl.when(j == 0)
    def _():
        for m_ref, l_ref, a_ref in ((m1_ref, l1_ref, a1_ref), (m2_ref, l2_ref, a2_ref)):
            m_ref[...] = jnp.full(m_ref.shape, NEG_INF, F32)
            l_ref[...] = jnp.zeros(l_ref.shape, F32)
            a_ref[...] = jnp.zeros(a_ref.shape, F32)

    qa, qb = _split_q(q_ref[...], DIFF_DIM ** -0.5)
    k = k_ref[...].astype(BF16)
    v = v_ref[...].astype(BF16)
    qpos = qi_ref[p] * bq + lax.broadcasted_iota(jnp.int32, (bq, 1), 0)
    kpos = j * bk + lax.broadcasted_iota(jnp.int32, (1, bk), 1)
    dist = jnp.abs(qpos - kpos).astype(F32)
    bias = jnp.where(_chunk_mask(qpos, kpos), -slope_ref[hd] * dist, NEG_INF)
    for qx, m_ref, l_ref, a_ref in ((qa, m1_ref, l1_ref, a1_ref), (qb, m2_ref, l2_ref, a2_ref)):
        s = _nt_dot(qx, k) + bias
        m_old = m_ref[...]
        m_new = jnp.maximum(m_old, jnp.max(s, axis=1, keepdims=True))
        alpha = jnp.exp(m_old - m_new)
        e = jnp.exp(s - m_new)
        l_ref[...] = alpha * l_ref[...] + jnp.sum(e, axis=1, keepdims=True)
        a_ref[...] = alpha * a_ref[...] + jnp.dot(e.astype(BF16), v, preferred_element_type=F32)
        m_ref[...] = m_new

    @pl.when(last_ref[p] == 1)
    def _():
        o = a1_ref[...] / l1_ref[...] - lam_ref[0] * (a2_ref[...] / l2_ref[...])
        o_ref[...] = _subln(o, g_ref[...], lambda_init).astype(o_ref.dtype)


def _flash_diff(h, slopes, lam, g, lambda_init, tp, bq, bk):
    qi, kj, last = _tri_pairs(tp, bq, bk)
    smem = pl.BlockSpec(memory_space=pltpu.SMEM)
    grid_spec = pltpu.PrefetchScalarGridSpec(
        num_scalar_prefetch=3,
        grid=(HEADS, qi.shape[0]),
        in_specs=[smem, smem,
                  pl.BlockSpec((bq, HEAD_W), lambda h, p, qi, kj, la: (qi[p], C_DQ // HEAD_W + h)),
                  pl.BlockSpec((bk, HEAD_W), lambda h, p, qi, kj, la: (kj[p], C_DK // HEAD_W + h)),
                  pl.BlockSpec((bk, HEAD_W), lambda h, p, qi, kj, la: (kj[p], C_DV // HEAD_W + h)),
                  pl.BlockSpec((1, HEAD_W), lambda h, p, qi, kj, la: (0, 0))],
        out_specs=pl.BlockSpec((bq, HEAD_W), lambda h, p, qi, kj, la: (qi[p], h)),
        scratch_shapes=[pltpu.VMEM((bq, 1), F32), pltpu.VMEM((bq, 1), F32), pltpu.VMEM((bq, HEAD_W), F32),
                        pltpu.VMEM((bq, 1), F32), pltpu.VMEM((bq, 1), F32), pltpu.VMEM((bq, HEAD_W), F32)],
    )
    return pl.pallas_call(
        functools.partial(_flash_diff_kernel, lambda_init=lambda_init),
        grid_spec=grid_spec,
        out_shape=jax.ShapeDtypeStruct((tp, HEADS * HEAD_W), BF16),
        compiler_params=_params(("parallel", "arbitrary")),
        name="flash_diff",
    )(qi, kj, last, slopes, lam, h, h, h, g)


def _samp_diff_kernel(slope_ref, lam_ref, q_ref, k_ref, v_ref, g_ref, o_ref, *, past, lambda_init):
    hd = pl.program_id(1)
    ts = q_ref.shape[0]
    tk = k_ref.shape[1]
    qa, qb = _split_q(q_ref[...], DIFF_DIM ** -0.5)
    k = k_ref[0]
    v = v_ref[0]
    qpos = past + lax.broadcasted_iota(jnp.int32, (ts, 1), 0)
    kpos = lax.broadcasted_iota(jnp.int32, (1, tk), 1)
    dist = jnp.abs(qpos - kpos).astype(F32)
    ok = _chunk_mask(qpos, kpos) & (kpos < past + ts)
    bias = jnp.where(ok, -slope_ref[hd] * dist, NEG_INF)
    outs = []
    for qx in (qa, qb):
        s = _nt_dot(qx, k) + bias
        e = jnp.exp(s - jnp.max(s, axis=1, keepdims=True))
        pv = jnp.dot(e.astype(BF16), v, preferred_element_type=F32)
        outs.append(pv / jnp.sum(e, axis=1, keepdims=True))
    o = outs[0] - lam_ref[0] * outs[1]
    o_ref[...] = _subln(o, g_ref[...], lambda_init).astype(o_ref.dtype)


def _samp_diff(h, kfull, vfull, slopes, lam, g, lambda_init, tp, nb, ts, past):
    r0 = tp // ts
    tk = kfull.shape[1]
    smem = pl.BlockSpec(memory_space=pltpu.SMEM)
    return pl.pallas_call(
        functools.partial(_samp_diff_kernel, past=past, lambda_init=lambda_init),
        grid=(nb, HEADS),
        in_specs=[smem, smem,
                  pl.BlockSpec((ts, HEAD_W), lambda b, h: (r0 + b, C_DQ // HEAD_W + h)),
                  pl.BlockSpec((1, tk, HEAD_W), lambda b, h: (b, 0, h)),
                  pl.BlockSpec((1, tk, HEAD_W), lambda b, h: (b, 0, h)),
                  pl.BlockSpec((1, HEAD_W), lambda b, h: (0, 0))],
        out_specs=pl.BlockSpec((ts, HEAD_W), lambda b, h: (b, h)),
        out_shape=jax.ShapeDtypeStruct((nb * ts, HEADS * HEAD_W), BF16),
        compiler_params=_params(("parallel", "parallel")),
        name="samp_diff",
    )(slopes, lam, h, kfull, vfull, g)


def _gelu(x):
    return 0.5 * x * (1.0 + jnp.tanh(math.sqrt(2.0 / math.pi) * (x + 0.044715 * (x * x * x))))


def _chunk_mlp_kernel(u_ref, v_ref, g_ref, b_ref, w_ref, bs_ref, y_ref, mv_ref):
    u = _gelu(u_ref[...])
    gv = _gelu(v_ref[...])
    mu = jnp.mean(gv, axis=-1, keepdims=True)
    var = jnp.mean(jnp.square(gv - mu), axis=-1, keepdims=True)
    v = (gv - mu) * lax.rsqrt(var + 1e-5) * g_ref[...] + b_ref[...]
    mv_ref[...] = v
    vb = v.astype(BF16)
    bs = bs_ref[...]
    for gi in range(GROUP_W // LANES):
        c0 = gi * LANES
        s = jnp.dot(w_ref[gi], vb[:, c0:c0 + LANES], preferred_element_type=F32) + bs[:, gi:gi + 1]
        y_ref[:, c0:c0 + LANES] = (u[:, c0:c0 + LANES] * s).astype(y_ref.dtype)


def _chunk_mlp(h, g, b, w_tril, bs_t, row0, nblk, ln, name):
    r0 = row0 // ln
    cw = GROUP_W
    return pl.pallas_call(
        _chunk_mlp_kernel,
        grid=(nblk,),
        in_specs=[pl.BlockSpec((ln, cw), lambda i: (r0 + i, C_DU // cw)),
                  pl.BlockSpec((ln, cw), lambda i: (r0 + i, C_DVV // cw)),
                  pl.BlockSpec((1, cw), lambda i: (0, 0)),
                  pl.BlockSpec((1, cw), lambda i: (0, 0)),
                  pl.BlockSpec(w_tril.shape, lambda i: (0, 0, 0)),
                  pl.BlockSpec(bs_t.shape, lambda i: (0, 0))],
        out_specs=[pl.BlockSpec((ln, cw), lambda i: (i, 0)),
                   pl.BlockSpec((ln, cw), lambda i: (i, 0))],
        out_shape=[jax.ShapeDtypeStruct((nblk * ln, cw), BF16),
                   jax.ShapeDtypeStruct((nblk * ln, cw), F32)],
        compiler_params=_params(("parallel",)),
        name=name,
    )(h, h, g, b, w_tril, bs_t)


def _mm4_kernel(a_ref, b_ref, c_ref, d_ref, w_ref, o_ref):
    gw = a_ref.shape[1]
    acc = jnp.dot(a_ref[...], w_ref[0:gw, :], preferred_element_type=F32)
    acc += jnp.dot(b_ref[...], w_ref[gw:2 * gw, :], preferred_element_type=F32)
    acc += jnp.dot(c_ref[...], w_ref[2 * gw:3 * gw, :], preferred_element_type=F32)
    acc += jnp.dot(d_ref[...], w_ref[3 * gw:4 * gw, :], preferred_element_type=F32)
    o_ref[...] = acc


def _out_proj(ya, yb, yc, yd, w, bm, bn):
    t, gw = ya.shape
    n = w.shape[1]
    xs = pl.BlockSpec((bm, gw), lambda i, j: (i, 0))
    return pl.pallas_call(
        _mm4_kernel,
        grid=(t // bm, n // bn),
        in_specs=[xs, xs, xs, xs, pl.BlockSpec((4 * gw, bn), lambda i, j: (0, j))],
        out_specs=pl.BlockSpec((bm, bn), lambda i, j: (i, j)),
        out_shape=jax.ShapeDtypeStruct((t, n), F32),
        compiler_params=_params(("parallel", "arbitrary")),
        name="out_proj",
    )(ya, yb, yc, yd, w)


def _layernorm_rows(z, g, b):
    mu = jnp.mean(z, axis=-1, keepdims=True)
    var = jnp.mean(jnp.square(z - mu), axis=-1, keepdims=True)
    return (z - mu) * lax.rsqrt(var + 1e-5) * g + b


def _ln_res_kernel(x_ref, y_ref, g_ref, b_ref, o_ref, ob_ref, *, alpha, pair):
    y = y_ref[...]
    if pair:
        d = x_ref.shape[1]
        y = y[:, :d] + y[:, d:]
    out = _layernorm_rows(alpha * x_ref[...] + y, g_ref[...], b_ref[...])
    o_ref[...] = out
    ob_ref[...] = out.astype(ob_ref.dtype)


def _ln_res(x, y, g, b, alpha, bm, name):
    t, d = x.shape
    pair = y.shape[1] == 2 * d
    return pl.pallas_call(
        functools.partial(_ln_res_kernel, alpha=alpha, pair=pair),
        grid=(t // bm,),
        in_specs=[pl.BlockSpec((bm, d), lambda i: (i, 0)),
                  pl.BlockSpec((bm, y.shape[1]), lambda i: (i, 0)),
                  pl.BlockSpec((1, d), lambda i: (0, 0)),
                  pl.BlockSpec((1, d), lambda i: (0, 0))],
        out_specs=[pl.BlockSpec((bm, d), lambda i: (i, 0)),
                   pl.BlockSpec((bm, d), lambda i: (i, 0))],
        out_shape=[jax.ShapeDtypeStruct((t, d), F32), jax.ShapeDtypeStruct((t, d), BF16)],
        compiler_params=_params(("parallel",)),
        name=name,
    )(x, y, g, b)


def _ffn_up_kernel(be_ref, nu_ref, x_ref, w1_ref, w3_ref, o_ref):
    used = pl.program_id(1) < nu_ref[0]

    @pl.when(used)
    def _():
        x = x_ref[...]
        a = jnp.dot(x, w1_ref[0], preferred_element_type=F32)
        b = jnp.dot(x, w3_ref[0], preferred_element_type=F32)
        o_ref[...] = (a * (1.0 / (1.0 + jnp.exp(-a))) * b).astype(o_ref.dtype)

    @pl.when(jnp.logical_not(used))
    def _():
        o_ref[...] = jnp.zeros(o_ref.shape, o_ref.dtype)


def _ffn_up(x, w1, w3, blk_e, n_used, bm, bn):
    m, k = x.shape
    n = w1.shape[2]

    def xmap(j, i, be, nu):
        return (jnp.minimum(i, nu[0] - 1), 0)

    def wmap(j, i, be, nu):
        return (be[jnp.minimum(i, nu[0] - 1)], 0, j)

    def omap(j, i, be, nu):
        return (i, j)

    grid_spec = pltpu.PrefetchScalarGridSpec(
        num_scalar_prefetch=2,
        grid=(n // bn, m // bm),
        in_specs=[pl.BlockSpec((bm, k), xmap),
                  pl.BlockSpec((1, k, bn), wmap),
                  pl.BlockSpec((1, k, bn), wmap)],
        out_specs=pl.BlockSpec((bm, bn), omap),
    )
    return pl.pallas_call(
        _ffn_up_kernel,
        grid_spec=grid_spec,
        out_shape=jax.ShapeDtypeStruct((m, n), BF16),
        compiler_params=_params(("parallel", "arbitrary")),
        name="ffn_up",
    )(blk_e, n_used, x, w1, w3)


def _ffn_down_kernel(be_ref, nu_ref, h_ref, w_ref, g_ref, o_ref):
    kk = pl.program_id(1)

    used = pl.program_id(0) < nu_ref[0]

    @pl.when(jnp.logical_not(used) & (kk == 0))
    def _():
        o_ref[...] = jnp.zeros(o_ref.shape, o_ref.dtype)

    @pl.when(used)
    def _():
        part = jnp.dot(h_ref[...], w_ref[0], preferred_element_type=F32)

        @pl.when(kk == 0)
        def _():
            o_ref[...] = part

        @pl.when(kk > 0)
        def _():
            o_ref[...] += part

        @pl.when(kk == pl.num_programs(1) - 1)
        def _():
            o_ref[...] = o_ref[...] * g_ref[...]


def _ffn_down(hm, w2, gate, blk_e, n_used, bm, bk):
    m, kdim = hm.shape
    n = w2.shape[2]

    nk = kdim // bk

    def hmap(i, kk, be, nu):
        return (jnp.minimum(i, nu[0] - 1), jnp.where(i < nu[0], kk, nk - 1))

    def wmap(i, kk, be, nu):
        return (be[jnp.minimum(i, nu[0] - 1)], jnp.where(i < nu[0], kk, nk - 1), 0)

    def gmap(i, kk, be, nu):
        return (jnp.minimum(i, nu[0] - 1), 0)

    grid_spec = pltpu.PrefetchScalarGridSpec(
        num_scalar_prefetch=2,
        grid=(m // bm, nk),
        in_specs=[pl.BlockSpec((bm, bk), hmap),
                  pl.BlockSpec((1, bk, n), wmap),
                  pl.BlockSpec((bm, 1), gmap)],
        out_specs=pl.BlockSpec((bm, n), lambda i, kk, be, nu: (i, 0)),
    )
    return pl.pallas_call(
        _ffn_down_kernel,
        grid_spec=grid_spec,
        out_shape=jax.ShapeDtypeStruct((m, n), F32),
        compiler_params=_params(("arbitrary", "arbitrary")),
        name="ffn_down",
    )(blk_e, n_used, hm, w2, gate)


def _router_kernel(x_ref, w_ref, b_ref, idx_ref, gate_ref, *, n_experts):
    logits = jnp.dot(x_ref[...], w_ref[...], preferred_element_type=F32,
                     precision=lax.Precision.HIGHEST) + b_ref[...]
    lane = lax.broadcasted_iota(jnp.int32, logits.shape, 1)
    logits = jnp.where(lane < n_experts, logits, -jnp.inf)
    v1 = jnp.max(logits, axis=1, keepdims=True)
    i1 = jnp.min(jnp.where(logits == v1, lane, LANES), axis=1, keepdims=True)
    rest = jnp.where(lane == i1, -jnp.inf, logits)
    v2 = jnp.max(rest, axis=1, keepdims=True)
    i2 = jnp.min(jnp.where(rest == v2, lane, LANES), axis=1, keepdims=True)
    e2 = jnp.exp(v2 - v1)
    g1 = 1.0 / (1.0 + e2)
    g2 = e2 / (1.0 + e2)
    idx_ref[...] = jnp.where(lane == 0, i1, jnp.where(lane == 1, i2, 0))
    gate_ref[...] = jnp.where(lane == 0, g1, jnp.where(lane == 1, g2, 0.0))


def _router(x, w_pad, b_pad, n_experts, bm):
    t, d = x.shape
    return pl.pallas_call(
        functools.partial(_router_kernel, n_experts=n_experts),
        grid=(t // bm,),
        in_specs=[pl.BlockSpec((bm, d), lambda i: (i, 0)),
                  pl.BlockSpec((d, LANES), lambda i: (0, 0)),
                  pl.BlockSpec((1, LANES), lambda i: (0, 0))],
        out_specs=[pl.BlockSpec((bm, LANES), lambda i: (i, 0)),
                   pl.BlockSpec((bm, LANES), lambda i: (i, 0))],
        out_shape=[jax.ShapeDtypeStruct((t, LANES), jnp.int32), jax.ShapeDtypeStruct((t, LANES), F32)],
        compiler_params=_params(("parallel",)),
        name="router",
    )(x, w_pad, b_pad)


def _gather_kernel(idx_ref, src_ref, dst_ref, sem, *, rows):
    base = pl.program_id(0) * rows

    def row_copy(r):
        return pltpu.make_async_copy(src_ref.at[idx_ref[base + r]], dst_ref.at[base + r], sem)

    def start(r, c):
        row_copy(r).start()
        return c

    def wait(r, c):
        row_copy(r).wait()
        return c

    lax.fori_loop(0, rows, start, 0)
    lax.fori_loop(0, rows, wait, 0)


def _gather_rows(src, idx, rows):
    n = idx.shape[0]
    width = src.shape[1]
    src = src.reshape(src.shape[0], 1, width)
    grid_spec = pltpu.PrefetchScalarGridSpec(
        num_scalar_prefetch=1,
        grid=(n // rows,),
        in_specs=[pl.BlockSpec(memory_space=pl.ANY)],
        out_specs=pl.BlockSpec(memory_space=pl.ANY),
        scratch_shapes=[pltpu.SemaphoreType.DMA(())],
    )
    return pl.pallas_call(
        functools.partial(_gather_kernel, rows=rows),
        grid_spec=grid_spec,
        out_shape=jax.ShapeDtypeStruct((n,) + src.shape[1:], src.dtype),
        compiler_params=_params(("arbitrary",)),
        name="gather_rows",
    )(idx, src).reshape(n, width)


def _moe(xf, xb, w_router, b_router, w1, w3, w2, ln_g, ln_b, alpha, bm_e, bm_ln):
    t, d = xf.shape
    n_e = w1.shape[0]
    w_pad = jnp.zeros((d, LANES), F32).at[:, :n_e].set(w_router)
    b_pad = jnp.zeros((1, LANES), F32).at[0, :n_e].set(b_router)
    idx, gates = _router(xf, w_pad, b_pad, n_e, bm_ln)
    flat_e = idx[:, :TOP_K].reshape(-1)
    flat_g = gates[:, :TOP_K].reshape(-1)
    a = t * TOP_K
    flat_t = jnp.arange(a, dtype=jnp.int32) // TOP_K
    onehot = (flat_e[:, None] == jnp.arange(n_e, dtype=jnp.int32)[None, :]).astype(jnp.int32)
    csum = jnp.cumsum(onehot, axis=0)
    rank = jnp.sum(csum * onehot, axis=1) - 1
    counts = csum[-1]
    padded = (counts + bm_e - 1) // bm_e * bm_e
    pend = jnp.cumsum(padded)
    pstart = pend - padded
    dest = (jnp.sum(pstart[None, :] * onehot, axis=1) + rank).astype(jnp.int32)
    nb = -(-a // bm_e) + n_e
    tok = jnp.zeros(nb * bm_e, jnp.int32).at[dest].set(flat_t)
    gate = jnp.zeros(nb * bm_e, F32).at[dest].set(flat_g)
    blk_e = jnp.minimum(jnp.searchsorted(pend, jnp.arange(nb, dtype=jnp.int32) * bm_e, side='right'),
                        n_e - 1).astype(jnp.int32)
    n_used = (pend[-1] // bm_e).astype(jnp.int32).reshape(1)
    xg = _gather_rows(xf, tok, bm_e).astype(BF16)
    hm = _ffn_up(xg, w1, w3, blk_e, n_used, bm_e, _pick(w1.shape[2], 1024, LANES))
    yb = _ffn_down(hm, w2, gate[:, None], blk_e, n_used, bm_e, _pick(w2.shape[1], 1024, LANES))
    yu = _gather_rows(yb, dest, _pick(a, 512, 8))
    return _ln_res(xf, yu.reshape(t, TOP_K * d), ln_g, ln_b, alpha, bm_ln, "ln2_moe")


def _dense_ffn(xf, xb, w1, w3, w2, ln_g, ln_b, alpha, bm, bm_ln):
    t, d = xf.shape
    nblk = t // bm
    blk_e = jnp.zeros((nblk,), jnp.int32)
    n_used = jnp.full((1,), nblk, jnp.int32)
    hm = _ffn_up(xb, w1[None], w3[None], blk_e, n_used, bm, _pick(w1.shape[1], 1024, LANES))
    ones = jnp.ones((t, 1), F32)
    f = _ffn_down(hm, w2[None], ones, blk_e, n_used, bm, _pick(w2.shape[0], 1024, LANES))
    return _ln_res(xf, f, ln_g, ln_b, alpha, bm_ln, "ln2_dense")


def _spread_rope_cols(w):
    z = jnp.zeros(w.shape[:-1] + (32,), w.dtype)
    return jnp.concatenate([w[..., :32], z, w[..., 32:], z], axis=-1)


def _prep_w_in(w):
    k0 = C_CQ + GROUP_W
    k1 = k0 + MLA_KV_RANK
    return jnp.concatenate([w[:, :k0], w[:, k1 + MLA_ROPE:], w[:, k0:k1],
                            _spread_rope_cols(w[:, k1:k1 + MLA_ROPE])], axis=1).astype(BF16)


def _prep_w_qb(w):
    r = w.shape[0]
    w = w.reshape(r, HEADS, MLA_NOPE + MLA_ROPE)
    out = jnp.concatenate([w[..., :MLA_NOPE], _spread_rope_cols(w[..., MLA_NOPE:])], axis=-1)
    return out.reshape(r, HEADS * 2 * LANES).astype(BF16)


def _prep_w_kvb(w):
    r = w.shape[0]
    w = w.reshape(r, HEADS, MLA_NOPE + HEAD_W)
    out = jnp.concatenate([w[..., :MLA_NOPE].reshape(r, -1), w[..., MLA_NOPE:].reshape(r, -1)], axis=-1)
    return out.astype(BF16)


def _rope_tables(pos):
    half = MLA_ROPE // 2
    freqs = ROPE_THETA ** (-jnp.arange(half, dtype=F32) / half)
    ang = pos.astype(F32)[:, None] * freqs
    cos, sin = jnp.cos(ang), jnp.sin(ang)
    z = jnp.zeros_like(cos)
    return (jnp.concatenate([cos, z, cos, z], axis=1), jnp.concatenate([-sin, z, sin, z], axis=1))


def _unspread(r):
    return jnp.concatenate([r[..., :32], r[..., 64:96]], axis=-1)


def _pad_keys(past, new, tk_pad):
    b, p, w = past.shape
    ts = new.shape[1]
    pad = jnp.zeros((b, tk_pad - p - ts, w), past.dtype)
    return jnp.concatenate([past, new, pad], axis=1)


def kernel(x_prompt, x_sample, cache_mla_ckv, cache_mla_krope, cache_diff_k, cache_diff_v, state_conv, w_in, conv_w, mla_q_norm, mla_w_qb, mla_kv_norm, mla_w_kvb, diff_lq1, diff_lk1, diff_lq2, diff_lk2, diff_subln, mlp_v_norm_g, mlp_v_norm_b, mlp_ws, mlp_bs, w_out, ln1_g, ln1_b, ln2_g, ln2_b, ffn_w1, ffn_w3, ffn_w2, moe_router, moe_router_b, moe_w1, moe_w3, moe_w2):
    nbp, tp, d = x_prompt.shape
    nbs, ts, _ = x_sample.shape
    depth = w_in.shape[0]
    past = cache_mla_ckv.shape[2]
    assert nbp == 1 and tp % MLP_CHUNK == 0 and ts % 16 == 0 and ts < MLP_CHUNK and tp % ts == 0
    t = tp + nbs * ts
    alpha = (2 * depth) ** 0.25
    tk_pad = -(-(past + ts) // LANES) * LANES

    bm_big = _pick(t, 1280, 16)
    bm_mid = _pick(t, 640, 16)
    bm_ln = _pick(t, 256, 16)
    bq = _pick(tp, 512, 64)
    bm_conv = _pick(tp, 256, 8)

    pos = jnp.concatenate([jnp.arange(tp, dtype=jnp.int32),
                           jnp.tile(past + jnp.arange(ts, dtype=jnp.int32), nbs)])
    cos, sin = _rope_tables(pos)
    slopes = 2.0 ** (-8.0 * jnp.arange(1, HEADS + 1, dtype=F32) / HEADS)

    xf = jnp.concatenate([x_prompt.reshape(tp, d), x_sample.reshape(nbs * ts, d)], axis=0)
    xb = xf.astype(BF16)
    outs = [[] for _ in range(11)]
    for l in range(depth):
        lambda_init = 0.8 - 0.6 * math.exp(-0.3 * l)
        h = _mm(xb, _prep_w_in(w_in[l]), bm_big, 896, F32, "in_proj")

        ya_p, conv_p = _conv_prompt(h, conv_w[l], tp, bm_conv)
        ya_s, conv_s = _conv_sample(h, state_conv[l], conv_w[l], tp, nbs, ts)

        q = _mla_q(h, mla_q_norm[l][None], _prep_w_qb(mla_w_qb[l]), cos, sin, bm_mid)
        w_kvb = _prep_w_kvb(mla_w_kvb[l])
        ckv, krf, kv, krb = _mla_kv(h, mla_kv_norm[l][None], w_kvb, cos, sin, bm_mid)
        yb_p = _flash_mla(q, kv, krb, tp, bq, bq)
        ckv_cache = cache_mla_ckv[l].reshape(nbs * past, MLA_KV_RANK).astype(BF16)
        kv_cache = _mm(ckv_cache, w_kvb, _pick(nbs * past, 1024, 16), w_kvb.shape[1], BF16, "kv_cache")
        kv_full = _pad_keys(kv_cache.reshape(nbs, past, -1), kv[tp:].reshape(nbs, ts, -1), tk_pad)
        kr_full = _pad_keys(_spread_rope_cols(cache_mla_krope[l]).astype(BF16),
                            krb[tp:].reshape(nbs, ts, LANES), tk_pad)
        yb_s = _samp_mla(q, kv_full, kr_full, tp, nbs, ts, past)

        lam = (jnp.exp(jnp.sum(diff_lq1[l] * diff_lk1[l])) - jnp.exp(jnp.sum(diff_lq2[l] * diff_lk2[l]))
               + lambda_init).reshape(1).astype(F32)
        sub_g = diff_subln[l][None]
        yc_p = _flash_diff(h, slopes, lam, sub_g, lambda_init, tp, bq, bq)
        dk_new = h[:, C_DK:C_DK + GROUP_W]
        dv_new = h[:, C_DV:C_DV + GROUP_W]
        dk_full = _pad_keys(cache_diff_k[l].reshape(nbs, past, GROUP_W).astype(BF16),
                            dk_new[tp:].reshape(nbs, ts, GROUP_W).astype(BF16), tk_pad)
        dv_full = _pad_keys(cache_diff_v[l].reshape(nbs, past, GROUP_W).astype(BF16),
                            dv_new[tp:].reshape(nbs, ts, GROUP_W).astype(BF16), tk_pad)
        yc_s = _samp_diff(h, dk_full, dv_full, slopes, lam, sub_g, lambda_init, tp, nbs, ts, past)

        tril = jnp.tril(jnp.ones((MLP_CHUNK, MLP_CHUNK), F32))
        w_sp = (mlp_ws[l] * tril).astype(BF16)
        mg, mb = mlp_v_norm_g[l][None], mlp_v_norm_b[l][None]
        yd_p, _ = _chunk_mlp(h, mg, mb, w_sp, jnp.transpose(mlp_bs[l]), 0, tp // MLP_CHUNK, MLP_CHUNK,
                             "chunk_mlp_prompt")
        yd_s, mv_s = _chunk_mlp(h, mg, mb, w_sp[:, :ts, :ts], jnp.transpose(mlp_bs[l][:, :ts]), tp, nbs, ts,
                                "chunk_mlp_sample")

        ya = jnp.concatenate([ya_p, ya_s], axis=0)
        yb = jnp.concatenate([yb_p, yb_s], axis=0)
        yc = jnp.concatenate([yc_p, yc_s], axis=0)
        yd = jnp.concatenate([yd_p, yd_s], axis=0)
        mix = _out_proj(ya, yb, yc, yd, w_out[l].astype(BF16), bm_big, 1024)
        xf, xb = _ln_res(xf, mix, ln1_g[l][None], ln1_b[l][None], alpha, bm_ln, "ln1")

        j = l // 2
        if l % 2 == 0:
            xf, xb = _dense_ffn(xf, xb, ffn_w1[j].astype(BF16), ffn_w3[j].astype(BF16), ffn_w2[j].astype(BF16),
                                ln2_g[l][None], ln2_b[l][None], alpha, bm_mid, bm_ln)
        else:
            xf, xb = _moe(xf, xb, moe_router[j], moe_router_b[j], moe_w1[j].astype(BF16),
                          moe_w3[j].astype(BF16), moe_w2[j].astype(BF16), ln2_g[l][None], ln2_b[l][None],
                          alpha, 512, bm_ln)

        kr_nat = _unspread(krf)
        new = (ckv[:tp].reshape(1, tp, -1), kr_nat[:tp].reshape(1, tp, -1),
               dk_new[:tp].reshape(1, tp, HEADS, HEAD_W), dv_new[:tp].reshape(1, tp, HEADS, HEAD_W),
               conv_p.reshape(1, CONV_W - 1, GROUP_W),
               ckv[tp:].reshape(nbs, ts, -1), kr_nat[tp:].reshape(nbs, ts, -1),
               dk_new[tp:].reshape(nbs, ts, HEADS, HEAD_W), dv_new[tp:].reshape(nbs, ts, HEADS, HEAD_W),
               conv_s, mv_s.reshape(nbs, ts, GROUP_W))
        for o, v in zip(outs, new):
            o.append(v)

    return (xf[:tp].reshape(1, tp, d), xf[tp:].reshape(nbs, ts, d)) + tuple(jnp.stack(o) for o in outs)
```

```python
import functools
import math

import numpy as np
import jax
import jax.numpy as jnp
from jax import lax
from jax.experimental import pallas as pl
from jax.experimental.pallas import tpu as pltpu

F32 = jnp.float32
BF16 = jnp.bfloat16

CHUNK_SHIFT = 6
GROUP_W = 1024
CONV_W = 3
HEADS = 8
MLA_NOPE = 128
MLA_ROPE = 64
MLA_KV_RANK = 512
HEAD_W = 128
DIFF_DIM = 64
MLP_CHUNK = 128
TOP_K = 2
ROPE_THETA = 10000.0
NEG_INF = -1e30
LANES = 128
VMEM_LIMIT = 56 * 1024 * 1024

C_AB, C_AC, C_AH, C_CQ = 0, 1024, 2048, 3072
C_DQ, C_DK, C_DV = 4096, 5120, 6144
C_DU, C_DVV = 7168, 8192
C_CKV = 9216
C_KR = 9728


def _params(sem, vmem=VMEM_LIMIT):
    return pltpu.CompilerParams(dimension_semantics=sem, vmem_limit_bytes=vmem)


def _pick(n, target, mult):
    best = None
    for d in range(mult, min(n, target) + 1, mult):
        if n % d == 0:
            best = d
    assert best is not None, (n, target, mult)
    return best


def _mm_kernel(x_ref, w_ref, o_ref):
    o_ref[...] = jnp.dot(x_ref[...], w_ref[...], preferred_element_type=F32).astype(o_ref.dtype)


def _mm(x, w, bm, bn, out_dtype, name):
    m, k = x.shape
    n = w.shape[1]
    return pl.pallas_call(
        _mm_kernel,
        grid=(m // bm, n // bn),
        in_specs=[pl.BlockSpec((bm, k), lambda i, j: (i, 0)),
                  pl.BlockSpec((k, bn), lambda i, j: (0, j))],
        out_specs=pl.BlockSpec((bm, bn), lambda i, j: (i, j)),
        out_shape=jax.ShapeDtypeStruct((m, n), out_dtype),
        compiler_params=_params(("parallel", "arbitrary")),
        name=name,
    )(x, w)


def _conv_prompt_kernel(b_ref, c_ref, h_ref, cp_ref, hp_ref, w_ref, y_ref, st_ref, zz_ref):
    i = pl.program_id(0)
    bm = b_ref.shape[0]
    z = c_ref[...] * h_ref[...]
    zprev = cp_ref[...] * hp_ref[...]
    zz_ref[0:8, :] = jnp.where(i == 0, 0.0, zprev)
    zz_ref[8:, :] = z
    w = w_ref[...]
    y = w[0:1] * zz_ref[pl.ds(6, bm), :] + w[1:2] * zz_ref[pl.ds(7, bm), :] + w[2:3] * z
    y_ref[...] = (b_ref[...] * y).astype(y_ref.dtype)
    st_ref[...] = z[bm - 8:, :]


def _conv_prompt(h, conv_w, tp, bm):
    nb = tp // bm
    r8 = bm // 8
    cw = GROUP_W
    y, st = pl.pallas_call(
        _conv_prompt_kernel,
        grid=(nb,),
        in_specs=[pl.BlockSpec((bm, cw), lambda i: (i, C_AB // cw)),
                  pl.BlockSpec((bm, cw), lambda i: (i, C_AC // cw)),
                  pl.BlockSpec((bm, cw), lambda i: (i, C_AH // cw)),
                  pl.BlockSpec((8, cw), lambda i: (jnp.maximum(i * r8 - 1, 0), C_AC // cw)),
                  pl.BlockSpec((8, cw), lambda i: (jnp.maximum(i * r8 - 1, 0), C_AH // cw)),
                  pl.BlockSpec((CONV_W, cw), lambda i: (0, 0))],
        out_specs=[pl.BlockSpec((bm, cw), lambda i: (i, 0)),
                   pl.BlockSpec((8, cw), lambda i: (0, 0))],
        out_shape=[jax.ShapeDtypeStruct((tp, cw), BF16),
                   jax.ShapeDtypeStruct((8, cw), F32)],
        scratch_shapes=[pltpu.VMEM((bm + 8, cw), F32)],
        compiler_params=_params(("arbitrary",)),
        name="conv_prompt",
    )(h, h, h, h, h, conv_w)
    return y, st[8 - (CONV_W - 1):]


def _conv_sample_kernel(b_ref, c_ref, h_ref, past_ref, w_ref, y_ref, st_ref, zz_ref):
    ts = b_ref.shape[0]
    z = c_ref[...] * h_ref[...]
    zz_ref[pl.ds(6, 2), :] = past_ref[0]
    zz_ref[pl.ds(8, ts), :] = z
    w = w_ref[...]
    y = w[0:1] * zz_ref[pl.ds(6, ts), :] + w[1:2] * zz_ref[pl.ds(7, ts), :] + w[2:3] * z
    y_ref[...] = (b_ref[...] * y).astype(y_ref.dtype)
    st_ref[0] = zz_ref[pl.ds(ts + 6, 2), :]


def _conv_sample(h, past, conv_w, tp, nb, ts):
    cw = GROUP_W
    r0 = tp // ts
    return pl.pallas_call(
        _conv_sample_kernel,
        grid=(nb,),
        in_specs=[pl.BlockSpec((ts, cw), lambda b: (r0 + b, C_AB // cw)),
                  pl.BlockSpec((ts, cw), lambda b: (r0 + b, C_AC // cw)),
                  pl.BlockSpec((ts, cw), lambda b: (r0 + b, C_AH // cw)),
                  pl.BlockSpec((1, CONV_W - 1, cw), lambda b: (b, 0, 0)),
                  pl.BlockSpec((CONV_W, cw), lambda b: (0, 0))],
        out_specs=[pl.BlockSpec((ts, cw), lambda b: (b, 0)),
                   pl.BlockSpec((1, CONV_W - 1, cw), lambda b: (b, 0, 0))],
        out_shape=[jax.ShapeDtypeStruct((nb * ts, cw), BF16),
                   jax.ShapeDtypeStruct((nb, CONV_W - 1, cw), F32)],
        scratch_shapes=[pltpu.VMEM((ts + 8, cw), F32)],
        compiler_params=_params(("parallel",)),
        name="conv_sample",
    )(h, h, h, past, conv_w)


def _rope_spread(r, cos, sin):
    return r * cos + pltpu.roll(r, 64, 1) * sin


def _mla_q_kernel(x_ref, g_ref, w_ref, cos_ref, sin_ref, o_ref, *, scale):
    x = x_ref[...]
    cn = x * lax.rsqrt(jnp.mean(x * x, axis=-1, keepdims=True) + 1e-6) * g_ref[...]
    q = jnp.dot(cn.astype(BF16), w_ref[...], preferred_element_type=F32)
    cos = cos_ref[...]
    sin = sin_ref[...]
    for hd in range(HEADS):
        c0 = hd * 2 * LANES
        o_ref[:, c0:c0 + LANES] = (q[:, c0:c0 + LANES] * scale).astype(o_ref.dtype)
        r = _rope_spread(q[:, c0 + LANES:c0 + 2 * LANES], cos, sin)
        o_ref[:, c0 + LANES:c0 + 2 * LANES] = (r * scale).astype(o_ref.dtype)


def _mla_q(h, g, w, cos, sin, bm):
    t = h.shape[0]
    n = w.shape[1]
    scale = (MLA_NOPE + MLA_ROPE) ** -0.5
    return pl.pallas_call(
        functools.partial(_mla_q_kernel, scale=scale),
        grid=(t // bm,),
        in_specs=[pl.BlockSpec((bm, GROUP_W), lambda i: (i, C_CQ // GROUP_W)),
                  pl.BlockSpec((1, GROUP_W), lambda i: (0, 0)),
                  pl.BlockSpec(w.shape, lambda i: (0, 0)),
                  pl.BlockSpec((bm, LANES), lambda i: (i, 0)),
                  pl.BlockSpec((bm, LANES), lambda i: (i, 0))],
        out_specs=pl.BlockSpec((bm, n), lambda i: (i, 0)),
        out_shape=jax.ShapeDtypeStruct((t, n), BF16),
        compiler_params=_params(("parallel",)),
        name="mla_q",
    )(h, g, w, cos, sin)


def _mla_kv_kernel(x_ref, kr_ref, g_ref, w_ref, cos_ref, sin_ref, ckv_ref, krf_ref, kv_ref, krb_ref):
    x = x_ref[...]
    cn = x * lax.rsqrt(jnp.mean(x * x, axis=-1, keepdims=True) + 1e-6) * g_ref[...]
    ckv_ref[...] = cn
    kv_ref[...] = jnp.dot(cn.astype(BF16), w_ref[...], preferred_element_type=F32).astype(kv_ref.dtype)
    r = _rope_spread(kr_ref[...], cos_ref[...], sin_ref[...])
    krf_ref[...] = r
    krb_ref[...] = r.astype(krb_ref.dtype)


def _mla_kv(h, g, w, cos, sin, bm):
    t = h.shape[0]
    n = w.shape[1]
    return pl.pallas_call(
        _mla_kv_kernel,
        grid=(t // bm,),
        in_specs=[pl.BlockSpec((bm, MLA_KV_RANK), lambda i: (i, C_CKV // MLA_KV_RANK)),
                  pl.BlockSpec((bm, LANES), lambda i: (i, C_KR // LANES)),
                  pl.BlockSpec((1, MLA_KV_RANK), lambda i: (0, 0)),
                  pl.BlockSpec(w.shape, lambda i: (0, 0)),
                  pl.BlockSpec((bm, LANES), lambda i: (i, 0)),
                  pl.BlockSpec((bm, LANES), lambda i: (i, 0))],
        out_specs=[pl.BlockSpec((bm, MLA_KV_RANK), lambda i: (i, 0)),
                   pl.BlockSpec((bm, LANES), lambda i: (i, 0)),
                   pl.BlockSpec((bm, n), lambda i: (i, 0)),
                   pl.BlockSpec((bm, LANES), lambda i: (i, 0))],
        out_shape=[jax.ShapeDtypeStruct((t, MLA_KV_RANK), F32),
                   jax.ShapeDtypeStruct((t, LANES), F32),
                   jax.ShapeDtypeStruct((t, n), BF16),
                   jax.ShapeDtypeStruct((t, LANES), BF16)],
        compiler_params=_params(("parallel",)),
        name="mla_kv",
    )(h, h, g, w, cos, sin)


def _nt_dot(a, b):
    return lax.dot_general(a, b, (((1,), (1,)), ((), ())), preferred_element_type=F32)


def _chunk_mask(qpos, kpos):
    return (kpos >> CHUNK_SHIFT) <= (qpos >> CHUNK_SHIFT)


def _tri_pairs(tp, bq, bk):
    qi, kj, last = [], [], []
    for i in range(tp // bq):
        q_hi = (i + 1) * bq - 1
        k_end = ((q_hi >> CHUNK_SHIFT) + 1) << CHUNK_SHIFT
        nj = -(-min(k_end, tp) // bk)
        for j in range(nj):
            qi.append(i)
            kj.append(j)
            last.append(1 if j == nj - 1 else 0)
    return (jnp.asarray(qi, jnp.int32), jnp.asarray(kj, jnp.int32), jnp.asarray(last, jnp.int32))


def _online_softmax_step(s, v, m_ref, l_ref, acc_ref, hd, c0):
    m_old = m_ref[hd]
    m_new = jnp.maximum(m_old, jnp.max(s, axis=1, keepdims=True))
    alpha = jnp.exp(m_old - m_new)
    e = jnp.exp(s - m_new)
    l_ref[hd] = alpha * l_ref[hd] + jnp.sum(e, axis=1, keepdims=True)
    acc_ref[:, c0:c0 + HEAD_W] = (alpha * acc_ref[:, c0:c0 + HEAD_W]
                                  + jnp.dot(e.astype(BF16), v, preferred_element_type=F32))
    m_ref[hd] = m_new


def _flash_mla_kernel(qi_ref, kj_ref, last_ref, q_ref, kn_ref, kr_ref, v_ref, o_ref, m_ref, l_ref, acc_ref):
    p = pl.program_id(0)
    bq = q_ref.shape[0]
    bk = kn_ref.shape[0]
    j = kj_ref[p]

    @pl.when(j == 0)
    def _():
        m_ref[...] = jnp.full(m_ref.shape, NEG_INF, F32)
        l_ref[...] = jnp.zeros(l_ref.shape, F32)
        acc_ref[...] = jnp.zeros(acc_ref.shape, F32)

    qpos = qi_ref[p] * bq + lax.broadcasted_iota(jnp.int32, (bq, 1), 0)
    kpos = j * bk + lax.broadcasted_iota(jnp.int32, (1, bk), 1)
    bias = jnp.where(_chunk_mask(qpos, kpos), 0.0, NEG_INF)
    kr = kr_ref[...]
    for hd in range(HEADS):
        c0 = hd * HEAD_W
        k = jnp.concatenate([kn_ref[:, c0:c0 + HEAD_W], kr], axis=1)
        s = _nt_dot(q_ref[:, 2 * c0:2 * c0 + 2 * LANES], k) + bias
        _online_softmax_step(s, v_ref[:, c0:c0 + HEAD_W], m_ref, l_ref, acc_ref, hd, c0)

    @pl.when(last_ref[p] == 1)
    def _():
        for hd in range(HEADS):
            c0 = hd * HEAD_W
            o_ref[:, c0:c0 + HEAD_W] = (acc_ref[:, c0:c0 + HEAD_W] / l_ref[hd]).astype(o_ref.dtype)


def _flash_mla(q, kv, krb, tp, bq, bk):
    qi, kj, last = _tri_pairs(tp, bq, bk)
    hw = HEADS * HEAD_W
    grid_spec = pltpu.PrefetchScalarGridSpec(
        num_scalar_prefetch=3,
        grid=(qi.shape[0],),
        in_specs=[pl.BlockSpec((bq, 2 * hw), lambda p, qi, kj, la: (qi[p], 0)),
                  pl.BlockSpec((bk, hw), lambda p, qi, kj, la: (kj[p], 0)),
                  pl.BlockSpec((bk, LANES), lambda p, qi, kj, la: (kj[p], 0)),
                  pl.BlockSpec((bk, hw), lambda p, qi, kj, la: (kj[p], 1))],
        out_specs=pl.BlockSpec((bq, hw), lambda p, qi, kj, la: (qi[p], 0)),
        scratch_shapes=[pltpu.VMEM((HEADS, bq, 1), F32), pltpu.VMEM((HEADS, bq, 1), F32),
                        pltpu.VMEM((bq, hw), F32)],
    )
    return pl.pallas_call(
        _flash_mla_kernel,
        grid_spec=grid_spec,
        out_shape=jax.ShapeDtypeStruct((tp, hw), BF16),
        compiler_params=_params(("arbitrary",)),
        name="flash_mla",
    )(qi, kj, last, q, kv, krb, kv)


def _samp_mla_kernel(q_ref, kn_ref, kr_ref, v_ref, o_ref, *, past):
    ts = q_ref.shape[0]
    tk = kn_ref.shape[1]
    k = jnp.concatenate([kn_ref[0], kr_ref[0]], axis=1)
    s = _nt_dot(q_ref[...], k)
    qpos = past + lax.broadcasted_iota(jnp.int32, (ts, 1), 0)
    kpos = lax.broadcasted_iota(jnp.int32, (1, tk), 1)
    s = jnp.where(_chunk_mask(qpos, kpos) & (kpos < past + ts), s, NEG_INF)
    e = jnp.exp(s - jnp.max(s, axis=1, keepdims=True))
    pv = jnp.dot(e.astype(BF16), v_ref[0], preferred_element_type=F32)
    o_ref[...] = (pv / jnp.sum(e, axis=1, keepdims=True)).astype(o_ref.dtype)


def _samp_mla(q, kvfull, krfull, tp, nb, ts, past):
    r0 = tp // ts
    tk = kvfull.shape[1]
    return pl.pallas_call(
        functools.partial(_samp_mla_kernel, past=past),
        grid=(nb, HEADS),
        in_specs=[pl.BlockSpec((ts, 2 * LANES), lambda b, h: (r0 + b, h)),
                  pl.BlockSpec((1, tk, LANES), lambda b, h: (b, 0, h)),
                  pl.BlockSpec((1, tk, LANES), lambda b, h: (b, 0, 0)),
                  pl.BlockSpec((1, tk, LANES), lambda b, h: (b, 0, HEADS + h))],
        out_specs=pl.BlockSpec((ts, LANES), lambda b, h: (b, h)),
        out_shape=jax.ShapeDtypeStruct((nb * ts, HEADS * LANES), BF16),
        compiler_params=_params(("parallel", "parallel")),
        name="samp_mla",
    )(q, kvfull, krfull, kvfull)


def _split_q(q, scale):
    lane = lax.broadcasted_iota(jnp.int32, q.shape, 1)
    qs = q * scale
    qa = jnp.where(lane < DIFF_DIM, qs, 0.0).astype(BF16)
    qb = jnp.where(lane >= DIFF_DIM, qs, 0.0).astype(BF16)
    return qa, qb


def _subln(o, g, lambda_init):
    o = o * lax.rsqrt(jnp.mean(o * o, axis=-1, keepdims=True) + 1e-5) * g
    return o * (1.0 - lambda_init)


def _flash_diff_kernel(qi_ref, kj_ref, last_ref, slope_ref, lam_ref, q_ref, k_ref, v_ref, g_ref, o_ref,
                       m_ref, l_ref, acc_ref, *, lambda_init):
    p = pl.program_id(0)
    bq = q_ref.shape[0]
    bk = k_ref.shape[0]
    hw = HEADS * HEAD_W
    j = kj_ref[p]

    @pl.when(j == 0)
    def _():
        m_ref[...] = jnp.full(m_ref.shape, NEG_INF, F32)
        l_ref[...] = jnp.zeros(l_ref.shape, F32)
        acc_ref[...] = jnp.zeros(acc_ref.shape, F32)

    qpos = qi_ref[p] * bq + lax.broadcasted_iota(jnp.int32, (bq, 1), 0)
    kpos = j * bk + lax.broadcasted_iota(jnp.int32, (1, bk), 1)
    ndist = jnp.where(_chunk_mask(qpos, kpos), -jnp.abs(qpos - kpos).astype(F32), NEG_INF)
    for hd in range(HEADS):
        c0 = hd * HEAD_W
        qa, qb = _split_q(q_ref[:, c0:c0 + HEAD_W], DIFF_DIM ** -0.5)
        k = k_ref[:, c0:c0 + HEAD_W].astype(BF16)
        v = v_ref[:, c0:c0 + HEAD_W].astype(BF16)
        bias = slope_ref[hd] * ndist
        _online_softmax_step(_nt_dot(qa, k) + bias, v, m_ref, l_ref, acc_ref, hd, c0)
        _online_softmax_step(_nt_dot(qb, k) + bias, v, m_ref, l_ref, acc_ref, HEADS + hd, hw + c0)

    @pl.when(last_ref[p] == 1)
    def _():
        for hd in range(HEADS):
            c0 = hd * HEAD_W
            o = (acc_ref[:, c0:c0 + HEAD_W] / l_ref[hd]
                 - lam_ref[0] * (acc_ref[:, hw + c0:hw + c0 + HEAD_W] / l_ref[HEADS + hd]))
            o_ref[:, c0:c0 + HEAD_W] = _subln(o, g_ref[...], lambda_init).astype(o_ref.dtype)


def _flash_diff(h, slopes, lam, g, lambda_init, tp, bq, bk):
    qi, kj, last = _tri_pairs(tp, bq, bk)
    hw = HEADS * HEAD_W
    smem = pl.BlockSpec(memory_space=pltpu.SMEM)
    grid_spec = pltpu.PrefetchScalarGridSpec(
        num_scalar_prefetch=3,
        grid=(qi.shape[0],),
        in_specs=[smem, smem,
                  pl.BlockSpec((bq, hw), lambda p, qi, kj, la: (qi[p], C_DQ // hw)),
                  pl.BlockSpec((bk, hw), lambda p, qi, kj, la: (kj[p], C_DK // hw)),
                  pl.BlockSpec((bk, hw), lambda p, qi, kj, la: (kj[p], C_DV // hw)),
                  pl.BlockSpec((1, HEAD_W), lambda p, qi, kj, la: (0, 0))],
        out_specs=pl.BlockSpec((bq, hw), lambda p, qi, kj, la: (qi[p], 0)),
        scratch_shapes=[pltpu.VMEM((2 * HEADS, bq, 1), F32), pltpu.VMEM((2 * HEADS, bq, 1), F32),
                        pltpu.VMEM((bq, 2 * hw), F32)],
    )
    return pl.pallas_call(
        functools.partial(_flash_diff_kernel, lambda_init=lambda_init),
        grid_spec=grid_spec,
        out_shape=jax.ShapeDtypeStruct((tp, hw), BF16),
        compiler_params=_params(("arbitrary",)),
        name="flash_diff",
    )(qi, kj, last, slopes, lam, h, h, h, g)


def _samp_diff_kernel(slope_ref, lam_ref, q_ref, k_ref, v_ref, g_ref, o_ref, *, past, lambda_init):
    hd = pl.program_id(1)
    ts = q_ref.shape[0]
    tk = k_ref.shape[1]
    qa, qb = _split_q(q_ref[...], DIFF_DIM ** -0.5)
    k = k_ref[0]
    v = v_ref[0]
    qpos = past + lax.broadcasted_iota(jnp.int32, (ts, 1), 0)
    kpos = lax.broadcasted_iota(jnp.int32, (1, tk), 1)
    dist = jnp.abs(qpos - kpos).astype(F32)
    ok = _chunk_mask(qpos, kpos) & (kpos < past + ts)
    bias = jnp.where(ok, -slope_ref[hd] * dist, NEG_INF)
    outs = []
    for qx in (qa, qb):
        s = _nt_dot(qx, k) + bias
        e = jnp.exp(s - jnp.max(s, axis=1, keepdims=True))
        pv = jnp.dot(e.astype(BF16), v, preferred_element_type=F32)
        outs.append(pv / jnp.sum(e, axis=1, keepdims=True))
    o = outs[0] - lam_ref[0] * outs[1]
    o_ref[...] = _subln(o, g_ref[...], lambda_init).astype(o_ref.dtype)


def _samp_diff(h, kfull, vfull, slopes, lam, g, lambda_init, tp, nb, ts, past):
    r0 = tp // ts
    tk = kfull.shape[1]
    smem = pl.BlockSpec(memory_space=pltpu.SMEM)
    return pl.pallas_call(
        functools.partial(_samp_diff_kernel, past=past, lambda_init=lambda_init),
        grid=(nb, HEADS),
        in_specs=[smem, smem,
                  pl.BlockSpec((ts, HEAD_W), lambda b, h: (r0 + b, C_DQ // HEAD_W + h)),
                  pl.BlockSpec((1, tk, HEAD_W), lambda b, h: (b, 0, h)),
                  pl.BlockSpec((1, tk, HEAD_W), lambda b, h: (b, 0, h)),
                  pl.BlockSpec((1, HEAD_W), lambda b, h: (0, 0))],
        out_specs=pl.BlockSpec((ts, HEAD_W), lambda b, h: (b, h)),
        out_shape=jax.ShapeDtypeStruct((nb * ts, HEADS * HEAD_W), BF16),
        compiler_params=_params(("parallel", "parallel")),
        name="samp_diff",
    )(slopes, lam, h, kfull, vfull, g)


def _layernorm_rows(z, g, b):
    mu = jnp.mean(z, axis=-1, keepdims=True)
    var = jnp.mean(jnp.square(z - mu), axis=-1, keepdims=True)
    return (z - mu) * lax.rsqrt(var + 1e-5) * g + b


def _gelu(x):
    return 0.5 * x * (1.0 + jnp.tanh(math.sqrt(2.0 / math.pi) * (x + 0.044715 * (x * x * x))))


def _chunk_mlp_kernel(u_ref, v_ref, g_ref, b_ref, w_ref, bs_ref, y_ref, *mv_ref):
    u = _gelu(u_ref[...])
    v = _layernorm_rows(_gelu(v_ref[...]), g_ref[...], b_ref[...])
    if mv_ref:
        mv_ref[0][...] = v
    vb = v.astype(BF16)
    bs = bs_ref[...]
    for gi in range(GROUP_W // LANES):
        c0 = gi * LANES
        s = jnp.dot(w_ref[gi], vb[:, c0:c0 + LANES], preferred_element_type=F32) + bs[:, gi:gi + 1]
        y_ref[:, c0:c0 + LANES] = (u[:, c0:c0 + LANES] * s).astype(y_ref.dtype)


def _chunk_mlp(h, g, b, w_tril, bs_t, row0, nblk, ln, with_mv, name):
    r0 = row0 // ln
    cw = GROUP_W
    n_out = 2 if with_mv else 1
    return pl.pallas_call(
        _chunk_mlp_kernel,
        grid=(nblk,),
        in_specs=[pl.BlockSpec((ln, cw), lambda i: (r0 + i, C_DU // cw)),
                  pl.BlockSpec((ln, cw), lambda i: (r0 + i, C_DVV // cw)),
                  pl.BlockSpec((1, cw), lambda i: (0, 0)),
                  pl.BlockSpec((1, cw), lambda i: (0, 0)),
                  pl.BlockSpec(w_tril.shape, lambda i: (0, 0, 0)),
                  pl.BlockSpec(bs_t.shape, lambda i: (0, 0))],
        out_specs=[pl.BlockSpec((ln, cw), lambda i: (i, 0)),
                   pl.BlockSpec((ln, cw), lambda i: (i, 0))][:n_out],
        out_shape=[jax.ShapeDtypeStruct((nblk * ln, cw), BF16),
                   jax.ShapeDtypeStruct((nblk * ln, cw), F32)][:n_out],
        compiler_params=_params(("parallel",)),
        name=name,
    )(h, h, g, b, w_tril, bs_t)


def _mm4_kernel(a_ref, b_ref, c_ref, d_ref, w_ref, o_ref):
    gw = a_ref.shape[1]
    acc = jnp.dot(a_ref[...], w_ref[0:gw, :], preferred_element_type=F32)
    acc += jnp.dot(b_ref[...], w_ref[gw:2 * gw, :], preferred_element_type=F32)
    acc += jnp.dot(c_ref[...], w_ref[2 * gw:3 * gw, :], preferred_element_type=F32)
    acc += jnp.dot(d_ref[...], w_ref[3 * gw:4 * gw, :], preferred_element_type=F32)
    o_ref[...] = acc


def _out_proj(ya, yb, yc, yd, w, bm, bn):
    t, gw = ya.shape
    n = w.shape[1]
    xs = pl.BlockSpec((bm, gw), lambda i, j: (i, 0))
    return pl.pallas_call(
        _mm4_kernel,
        grid=(t // bm, n // bn),
        in_specs=[xs, xs, xs, xs, pl.BlockSpec((4 * gw, bn), lambda i, j: (0, j))],
        out_specs=pl.BlockSpec((bm, bn), lambda i, j: (i, j)),
        out_shape=jax.ShapeDtypeStruct((t, n), F32),
        compiler_params=_params(("parallel", "arbitrary")),
        name="out_proj",
    )(ya, yb, yc, yd, w)


def _ln_res_kernel(x_ref, y_ref, g_ref, b_ref, o_ref, ob_ref, *, alpha):
    out = _layernorm_rows(alpha * x_ref[...] + y_ref[...], g_ref[...], b_ref[...])
    o_ref[...] = out
    ob_ref[...] = out.astype(ob_ref.dtype)


def _ln_res(x, y, g, b, alpha, bm, name):
    t, d = x.shape
    return pl.pallas_call(
        functools.partial(_ln_res_kernel, alpha=alpha),
        grid=(t // bm,),
        in_specs=[pl.BlockSpec((bm, d), lambda i: (i, 0)),
                  pl.BlockSpec((bm, d), lambda i: (i, 0)),
                  pl.BlockSpec((1, d), lambda i: (0, 0)),
                  pl.BlockSpec((1, d), lambda i: (0, 0))],
        out_specs=[pl.BlockSpec((bm, d), lambda i: (i, 0)),
                   pl.BlockSpec((bm, d), lambda i: (i, 0))],
        out_shape=[jax.ShapeDtypeStruct((t, d), F32), jax.ShapeDtypeStruct((t, d), BF16)],
        compiler_params=_params(("parallel",)),
        name=name,
    )(x, y, g, b)


def _ffn_up_kernel(be_ref, nu_ref, x_ref, w1_ref, w3_ref, o_ref):
    used = pl.program_id(1) < nu_ref[0]

    @pl.when(used)
    def _():
        x = x_ref[...]
        a = jnp.dot(x, w1_ref[0], preferred_element_type=F32)
        b = jnp.dot(x, w3_ref[0], preferred_element_type=F32)
        o_ref[...] = (a * (1.0 / (1.0 + jnp.exp(-a))) * b).astype(o_ref.dtype)

    @pl.when(jnp.logical_not(used))
    def _():
        o_ref[...] = jnp.zeros(o_ref.shape, o_ref.dtype)


def _ffn_up(x, w1, w3, blk_e, n_used, bm, bn):
    m, k = x.shape
    n = w1.shape[2]

    def xmap(j, i, be, nu):
        return (jnp.minimum(i, nu[0] - 1), 0)

    def wmap(j, i, be, nu):
        return (be[jnp.minimum(i, nu[0] - 1)], 0, j)

    def omap(j, i, be, nu):
        return (i, j)

    grid_spec = pltpu.PrefetchScalarGridSpec(
        num_scalar_prefetch=2,
        grid=(n // bn, m // bm),
        in_specs=[pl.BlockSpec((bm, k), xmap),
                  pl.BlockSpec((1, k, bn), wmap),
                  pl.BlockSpec((1, k, bn), wmap)],
        out_specs=pl.BlockSpec((bm, bn), omap),
    )
    return pl.pallas_call(
        _ffn_up_kernel,
        grid_spec=grid_spec,
        out_shape=jax.ShapeDtypeStruct((m, n), BF16),
        compiler_params=_params(("parallel", "arbitrary")),
        name="ffn_up",
    )(blk_e, n_used, x, w1, w3)


def _ffn_down_kernel(be_ref, nu_ref, h_ref, w_ref, o_ref):
    kk = pl.program_id(1)
    part = jnp.dot(h_ref[...], w_ref[0], preferred_element_type=F32)

    @pl.when(kk == 0)
    def _():
        o_ref[...] = part

    @pl.when(kk > 0)
    def _():
        o_ref[...] += part


def _ffn_down_slab_kernel(be_ref, nu_ref, h_ref, w_ref, g_ref, o_ref, acc_ref):
    kk = pl.program_id(1)
    bm = acc_ref.shape[0]
    used = pl.program_id(0) < nu_ref[0]
    final = kk == pl.num_programs(1) - 1

    @pl.when(used)
    def _():
        part = jnp.dot(h_ref[...], w_ref[0], preferred_element_type=F32)

        @pl.when(kk == 0)
        def _():
            acc_ref[...] = part

        @pl.when(kk > 0)
        def _():
            acc_ref[...] += part

    @pl.when(used & final)
    def _():
        acc_ref[...] = acc_ref[...] * g_ref[...]

        def row(r, c):
            o_ref[r] = acc_ref[pl.ds(r, 1), :]
            return c

        lax.fori_loop(0, bm, row, 0, unroll=8)

    @pl.when(jnp.logical_not(used) & final)
    def _():
        o_ref[...] = jnp.zeros(o_ref.shape, o_ref.dtype)


def _ffn_down(hm, w2, gate, blk_e, n_used, bm, bk):
    m, kdim = hm.shape
    n = w2.shape[2]
    nk = kdim // bk

    def hmap(i, kk, be, nu):
        return (jnp.minimum(i, nu[0] - 1), jnp.where(i < nu[0], kk, nk - 1))

    def wmap(i, kk, be, nu):
        return (be[jnp.minimum(i, nu[0] - 1)], jnp.where(i < nu[0], kk, nk - 1), 0)

    def gmap(i, kk, be, nu):
        return (jnp.minimum(i, nu[0] - 1), 0)

    in_specs = [pl.BlockSpec((bm, bk), hmap), pl.BlockSpec((1, bk, n), wmap)]
    if gate is None:
        body, args, scratch = _ffn_down_kernel, (hm, w2), []
        out_spec = pl.BlockSpec((bm, n), lambda i, kk, be, nu: (i, 0))
        out_shape = jax.ShapeDtypeStruct((m, n), F32)
    else:
        body, args, scratch = _ffn_down_slab_kernel, (hm, w2, gate), [pltpu.VMEM((bm, n), F32)]
        in_specs.append(pl.BlockSpec((bm, 1), gmap))
        out_spec = pl.BlockSpec((bm, 1, n), lambda i, kk, be, nu: (i, 0, 0))
        out_shape = jax.ShapeDtypeStruct((m, 1, n), F32)
    grid_spec = pltpu.PrefetchScalarGridSpec(
        num_scalar_prefetch=2, grid=(m // bm, nk), in_specs=in_specs, out_specs=out_spec,
        scratch_shapes=scratch)
    return pl.pallas_call(
        body,
        grid_spec=grid_spec,
        out_shape=out_shape,
        compiler_params=_params(("arbitrary", "arbitrary")),
        name="ffn_down",
    )(blk_e, n_used, *args)


def _router_kernel(x_ref, w_ref, b_ref, idx_ref, gate_ref, cnt_ref, base_ref, *, n_experts):
    @pl.when(pl.program_id(0) == 0)
    def _():
        base_ref[...] = jnp.zeros(base_ref.shape, F32)

    bm = x_ref.shape[0]
    logits = jnp.dot(x_ref[...], w_ref[...], preferred_element_type=F32,
                     precision=lax.Precision.HIGHEST) + b_ref[...]
    lane = lax.broadcasted_iota(jnp.int32, logits.shape, 1)
    logits = jnp.where(lane < n_experts, logits, -jnp.inf)
    v1 = jnp.max(logits, axis=1, keepdims=True)
    i1 = jnp.min(jnp.where(logits == v1, lane, LANES), axis=1, keepdims=True)
    rest = jnp.where(lane == i1, -jnp.inf, logits)
    v2 = jnp.max(rest, axis=1, keepdims=True)
    i2 = jnp.min(jnp.where(rest == v2, lane, LANES), axis=1, keepdims=True)
    e2 = jnp.exp(v2 - v1)
    g1 = 1.0 / (1.0 + e2)
    g2 = e2 / (1.0 + e2)
    oh1 = (lane == i1).astype(F32)
    oh2 = (lane == i2).astype(F32)
    both = oh1 + oh2
    row = lax.broadcasted_iota(jnp.int32, (bm, bm), 0)
    col = lax.broadcasted_iota(jnp.int32, (bm, bm), 1)
    tril = (col <= row).astype(BF16)
    incl = jnp.dot(tril, both.astype(BF16), preferred_element_type=F32)
    before = base_ref[...] + incl - both
    r1 = jnp.sum(oh1 * before, axis=1, keepdims=True).astype(jnp.int32)
    r2 = jnp.sum(oh2 * before, axis=1, keepdims=True).astype(jnp.int32)
    total = base_ref[...] + incl[bm - 1:bm, :]
    base_ref[...] = total
    cnt_ref[...] = jnp.broadcast_to(total, cnt_ref.shape)
    idx_ref[...] = jnp.where(lane == 0, i1, jnp.where(lane == 1, i2,
                             jnp.where(lane == 2, r1, jnp.where(lane == 3, r2, 0))))
    gate_ref[...] = jnp.where(lane == 0, g1, jnp.where(lane == 1, g2, 0.0))


def _router(x, w_pad, b_pad, n_experts, bm):
    t, d = x.shape
    return pl.pallas_call(
        functools.partial(_router_kernel, n_experts=n_experts),
        grid=(t // bm,),
        in_specs=[pl.BlockSpec((bm, d), lambda i: (i, 0)),
                  pl.BlockSpec((d, LANES), lambda i: (0, 0)),
                  pl.BlockSpec((1, LANES), lambda i: (0, 0))],
        out_specs=[pl.BlockSpec((bm, LANES), lambda i: (i, 0)),
                   pl.BlockSpec((bm, LANES), lambda i: (i, 0)),
                   pl.BlockSpec((8, LANES), lambda i: (0, 0))],
        out_shape=[jax.ShapeDtypeStruct((t, LANES), jnp.int32), jax.ShapeDtypeStruct((t, LANES), F32),
                   jax.ShapeDtypeStruct((8, LANES), F32)],
        scratch_shapes=[pltpu.VMEM((1, LANES), F32)],
        compiler_params=_params(("arbitrary",)),
        name="router",
    )(x, w_pad, b_pad)


def _start_row_gathers(src_ref, idx_ref, first, buf_ref, sem, rows):
    def body(r, c):
        pltpu.make_async_copy(src_ref.at[idx_ref[first + r]], buf_ref.at[r], sem).start()
        return c

    lax.fori_loop(0, rows, body, 0, unroll=4)


def _wait_row_gathers(src_ref, buf_ref, sem, rows):
    def body(r, c):
        pltpu.make_async_copy(src_ref.at[0], buf_ref.at[r], sem).wait()
        return c

    lax.fori_loop(0, rows, body, 0, unroll=4)


def _gather_x_kernel(idx_ref, src_ref, o_ref, buf_ref, stage_ref, sem):
    i = pl.program_id(0)
    rows = stage_ref.shape[0]
    slot = i % 2

    @pl.when(i == 0)
    def _():
        _start_row_gathers(src_ref, idx_ref, 0, buf_ref.at[0], sem.at[0], rows)

    @pl.when(i + 1 < pl.num_programs(0))
    def _():
        _start_row_gathers(src_ref, idx_ref, (i + 1) * rows, buf_ref.at[1 - slot], sem.at[1 - slot], rows)

    _wait_row_gathers(src_ref, buf_ref.at[slot], sem.at[slot], rows)

    def row(r, c):
        stage_ref[pl.ds(r, 1), :] = buf_ref[slot, r]
        return c

    lax.fori_loop(0, rows, row, 0, unroll=8)
    o_ref[...] = stage_ref[...].astype(o_ref.dtype)


def _gather_x(src, idx, rows):
    n = idx.shape[0]
    width = src.shape[2]
    grid_spec = pltpu.PrefetchScalarGridSpec(
        num_scalar_prefetch=1,
        grid=(n // rows,),
        in_specs=[pl.BlockSpec(memory_space=pl.ANY)],
        out_specs=pl.BlockSpec((rows, width), lambda i, idx: (i, 0)),
        scratch_shapes=[pltpu.VMEM((2, rows, 1, width), F32), pltpu.VMEM((rows, width), F32),
                        pltpu.SemaphoreType.DMA((2,))],
    )
    return pl.pallas_call(
        _gather_x_kernel,
        grid_spec=grid_spec,
        out_shape=jax.ShapeDtypeStruct((n, width), BF16),
        compiler_params=_params(("arbitrary",)),
        name="gather_x",
    )(idx, src)


def _moe_combine_kernel(d0_ref, d1_ref, x_ref, y_ref, g_ref, b_ref, o_ref, ob_ref, buf_ref, stage_ref, sem,
                        *, alpha):
    i = pl.program_id(0)
    rows = stage_ref.shape[0]
    slot = i % 2

    def start(blk, s):
        _start_row_gathers(y_ref, d0_ref, blk * rows, buf_ref.at[s, 0], sem.at[s], rows)
        _start_row_gathers(y_ref, d1_ref, blk * rows, buf_ref.at[s, 1], sem.at[s], rows)

    @pl.when(i == 0)
    def _():
        start(0, 0)

    @pl.when(i + 1 < pl.num_programs(0))
    def _():
        start(i + 1, 1 - slot)

    _wait_row_gathers(y_ref, buf_ref.at[slot, 0], sem.at[slot], rows)
    _wait_row_gathers(y_ref, buf_ref.at[slot, 1], sem.at[slot], rows)

    def row(r, c):
        stage_ref[pl.ds(r, 1), :] = buf_ref[slot, 0, r] + buf_ref[slot, 1, r]
        return c

    lax.fori_loop(0, rows, row, 0, unroll=8)
    out = _layernorm_rows(alpha * x_ref[...] + stage_ref[...], g_ref[...], b_ref[...])
    o_ref[...] = out
    ob_ref[...] = out.astype(ob_ref.dtype)


def _moe_combine(x, y_slabs, d0, d1, g, b, alpha, rows):
    t, d = x.shape
    grid_spec = pltpu.PrefetchScalarGridSpec(
        num_scalar_prefetch=2,
        grid=(t // rows,),
        in_specs=[pl.BlockSpec((rows, d), lambda i, d0, d1: (i, 0)),
                  pl.BlockSpec(memory_space=pl.ANY),
                  pl.BlockSpec((1, d), lambda i, d0, d1: (0, 0)),
                  pl.BlockSpec((1, d), lambda i, d0, d1: (0, 0))],
        out_specs=[pl.BlockSpec((rows, d), lambda i, d0, d1: (i, 0)),
                   pl.BlockSpec((rows, d), lambda i, d0, d1: (i, 0))],
        scratch_shapes=[pltpu.VMEM((2, 2, rows, 1, d), F32), pltpu.VMEM((rows, d), F32),
                        pltpu.SemaphoreType.DMA((2,))],
    )
    return pl.pallas_call(
        functools.partial(_moe_combine_kernel, alpha=alpha),
        grid_spec=grid_spec,
        out_shape=[jax.ShapeDtypeStruct((t, d), F32), jax.ShapeDtypeStruct((t, d), BF16)],
        compiler_params=_params(("arbitrary",)),
        name="moe_combine",
    )(d0, d1, x, y_slabs, g, b)


def _moe(xf, w_router, b_router, w1, w3, w2, ln_g, ln_b, alpha, bm_e, bm_ln):
    t, d = xf.shape
    n_e = w1.shape[0]
    w_pad = jnp.zeros((d, LANES), F32).at[:, :n_e].set(w_router)
    b_pad = jnp.zeros((1, LANES), F32).at[0, :n_e].set(b_router)
    idx, gates, cnt = _router(xf, w_pad, b_pad, n_e, bm_ln)
    a = t * TOP_K
    flat_e = idx[:, :TOP_K].reshape(-1)
    rank = idx[:, TOP_K:2 * TOP_K].reshape(-1)
    flat_g = gates[:, :TOP_K].reshape(-1)
    flat_t = jnp.arange(a, dtype=jnp.int32) // TOP_K
    counts = cnt[0, :n_e].astype(jnp.int32)
    padded = (counts + bm_e - 1) // bm_e * bm_e
    pend = jnp.cumsum(padded)
    pstart = pend - padded
    dest = (jnp.take(pstart, flat_e) + rank).astype(jnp.int32)
    nb = -(-a // bm_e) + n_e
    tok = jnp.zeros(nb * bm_e, jnp.int32).at[dest].set(flat_t)
    gate = jnp.zeros(nb * bm_e, F32).at[dest].set(flat_g)
    blk_e = jnp.minimum(jnp.searchsorted(pend, jnp.arange(nb, dtype=jnp.int32) * bm_e, side='right'),
                        n_e - 1).astype(jnp.int32)
    n_used = (pend[-1] // bm_e).astype(jnp.int32).reshape(1)
    xg = _gather_x(xf.reshape(t, 1, d), tok, bm_e)
    hm = _ffn_up(xg, w1, w3, blk_e, n_used, bm_e, _pick(w1.shape[2], 1024, LANES))
    yb = _ffn_down(hm, w2, gate[:, None], blk_e, n_used, bm_e, _pick(w2.shape[1], 1024, LANES))
    dest2 = dest.reshape(t, TOP_K)
    return _moe_combine(xf, yb, dest2[:, 0], dest2[:, 1], ln_g, ln_b, alpha, bm_ln)


def _dense_ffn(xf, xb, w1, w3, w2, ln_g, ln_b, alpha, bm, bm_ln):
    t, d = xf.shape
    nblk = t // bm
    blk_e = jnp.zeros((nblk,), jnp.int32)
    n_used = jnp.full((1,), nblk, jnp.int32)
    hm = _ffn_up(xb, w1[None], w3[None], blk_e, n_used, bm, _pick(w1.shape[1], 1024, LANES))
    f = _ffn_down(hm, w2[None], None, blk_e, n_used, bm, _pick(w2.shape[0], 1024, LANES))
    return _ln_res(xf, f, ln_g, ln_b, alpha, bm_ln, "ln2_dense")


def _spread_rope_cols(w):
    z = jnp.zeros(w.shape[:-1] + (32,), w.dtype)
    return jnp.concatenate([w[..., :32], z, w[..., 32:], z], axis=-1)


def _prep_w_in(w):
    k0 = C_CQ + GROUP_W
    k1 = k0 + MLA_KV_RANK
    return jnp.concatenate([w[:, :k0], w[:, k1 + MLA_ROPE:], w[:, k0:k1],
                            _spread_rope_cols(w[:, k1:k1 + MLA_ROPE])], axis=1).astype(BF16)


def _prep_w_qb(w):
    r = w.shape[0]
    w = w.reshape(r, HEADS, MLA_NOPE + MLA_ROPE)
    out = jnp.concatenate([w[..., :MLA_NOPE], _spread_rope_cols(w[..., MLA_NOPE:])], axis=-1)
    return out.reshape(r, HEADS * 2 * LANES).astype(BF16)


def _prep_w_kvb(w):
    r = w.shape[0]
    w = w.reshape(r, HEADS, MLA_NOPE + HEAD_W)
    out = jnp.concatenate([w[..., :MLA_NOPE].reshape(r, -1), w[..., MLA_NOPE:].reshape(r, -1)], axis=-1)
    return out.astype(BF16)


def _rope_tables(pos):
    half = MLA_ROPE // 2
    freqs = ROPE_THETA ** (-jnp.arange(half, dtype=F32) / half)
    ang = pos.astype(F32)[:, None] * freqs
    cos, sin = jnp.cos(ang), jnp.sin(ang)
    z = jnp.zeros_like(cos)
    return (jnp.concatenate([cos, z, cos, z], axis=1), jnp.concatenate([-sin, z, sin, z], axis=1))


def _unspread(r):
    return jnp.concatenate([r[..., :32], r[..., 64:96]], axis=-1)


def _pad_keys(past, new, tk_pad):
    b, p, w = past.shape
    ts = new.shape[1]
    pad = jnp.zeros((b, tk_pad - p - ts, w), past.dtype)
    return jnp.concatenate([past, new, pad], axis=1)


def kernel(x_prompt, x_sample, cache_mla_ckv, cache_mla_krope, cache_diff_k, cache_diff_v, state_conv, w_in, conv_w, mla_q_norm, mla_w_qb, mla_kv_norm, mla_w_kvb, diff_lq1, diff_lk1, diff_lq2, diff_lk2, diff_subln, mlp_v_norm_g, mlp_v_norm_b, mlp_ws, mlp_bs, w_out, ln1_g, ln1_b, ln2_g, ln2_b, ffn_w1, ffn_w3, ffn_w2, moe_router, moe_router_b, moe_w1, moe_w3, moe_w2):
    nbp, tp, d = x_prompt.shape
    nbs, ts, _ = x_sample.shape
    depth = w_in.shape[0]
    past = cache_mla_ckv.shape[2]
    assert nbp == 1 and tp % MLP_CHUNK == 0 and ts % 16 == 0 and ts < MLP_CHUNK and tp % ts == 0
    t = tp + nbs * ts
    alpha = (2 * depth) ** 0.25
    tk_pad = -(-(past + ts) // LANES) * LANES

    bm_big = _pick(t, 1280, 16)
    bm_mid = _pick(t, 640, 16)
    bm_ln = _pick(t, 256, 16)
    bq = _pick(tp, 512, 64)
    bm_conv = _pick(tp, 256, 8)

    pos = jnp.concatenate([jnp.arange(tp, dtype=jnp.int32),
                           jnp.tile(past + jnp.arange(ts, dtype=jnp.int32), nbs)])
    cos, sin = _rope_tables(pos)
    slopes = 2.0 ** (-8.0 * jnp.arange(1, HEADS + 1, dtype=F32) / HEADS)

    xf = jnp.concatenate([x_prompt.reshape(tp, d), x_sample.reshape(nbs * ts, d)], axis=0)
    xb = xf.astype(BF16)
    outs = [[] for _ in range(11)]
    for l in range(depth):
        lambda_init = 0.8 - 0.6 * math.exp(-0.3 * l)
        h = _mm(xb, _prep_w_in(w_in[l]), bm_big, 896, F32, "in_proj")

        ya_p, conv_p = _conv_prompt(h, conv_w[l], tp, bm_conv)
        ya_s, conv_s = _conv_sample(h, state_conv[l], conv_w[l], tp, nbs, ts)

        q = _mla_q(h, mla_q_norm[l][None], _prep_w_qb(mla_w_qb[l]), cos, sin, bm_mid)
        w_kvb = _prep_w_kvb(mla_w_kvb[l])
        ckv, krf, kv, krb = _mla_kv(h, mla_kv_norm[l][None], w_kvb, cos, sin, bm_mid)
        yb_p = _flash_mla(q, kv, krb, tp, bq, bq)
        ckv_cache = cache_mla_ckv[l].reshape(nbs * past, MLA_KV_RANK).astype(BF16)
        kv_cache = _mm(ckv_cache, w_kvb, _pick(nbs * past, 1024, 16), w_kvb.shape[1], BF16, "kv_cache")
        kv_full = _pad_keys(kv_cache.reshape(nbs, past, -1), kv[tp:].reshape(nbs, ts, -1), tk_pad)
        kr_full = _pad_keys(_spread_rope_cols(cache_mla_krope[l]).astype(BF16),
                            krb[tp:].reshape(nbs, ts, LANES), tk_pad)
        yb_s = _samp_mla(q, kv_full, kr_full, tp, nbs, ts, past)

        lam = (jnp.exp(jnp.sum(diff_lq1[l] * diff_lk1[l])) - jnp.exp(jnp.sum(diff_lq2[l] * diff_lk2[l]))
               + lambda_init).reshape(1).astype(F32)
        sub_g = diff_subln[l][None]
        yc_p = _flash_diff(h, slopes, lam, sub_g, lambda_init, tp, bq, bq)
        dk_new = h[:, C_DK:C_DK + GROUP_W]
        dv_new = h[:, C_DV:C_DV + GROUP_W]
        dk_full = _pad_keys(cache_diff_k[l].reshape(nbs, past, GROUP_W).astype(BF16),
                            dk_new[tp:].reshape(nbs, ts, GROUP_W).astype(BF16), tk_pad)
        dv_full = _pad_keys(cache_diff_v[l].reshape(nbs, past, GROUP_W).astype(BF16),
                            dv_new[tp:].reshape(nbs, ts, GROUP_W).astype(BF16), tk_pad)
        yc_s = _samp_diff(h, dk_full, dv_full, slopes, lam, sub_g, lambda_init, tp, nbs, ts, past)

        tril = jnp.tril(jnp.ones((MLP_CHUNK, MLP_CHUNK), F32))
        w_sp = (mlp_ws[l] * tril).astype(BF16)
        mg, mb = mlp_v_norm_g[l][None], mlp_v_norm_b[l][None]
        yd_p, = _chunk_mlp(h, mg, mb, w_sp, jnp.transpose(mlp_bs[l]), 0, tp // MLP_CHUNK, MLP_CHUNK, False,
                           "chunk_mlp_prompt")
        yd_s, mv_s = _chunk_mlp(h, mg, mb, w_sp[:, :ts, :ts], jnp.transpose(mlp_bs[l][:, :ts]), tp, nbs, ts,
                                True, "chunk_mlp_sample")

        ya = jnp.concatenate([ya_p, ya_s], axis=0)
        yb = jnp.concatenate([yb_p, yb_s], axis=0)
        yc = jnp.concatenate([yc_p, yc_s], axis=0)
        yd = jnp.concatenate([yd_p, yd_s], axis=0)
        mix = _out_proj(ya, yb, yc, yd, w_out[l].astype(BF16), bm_big, 1024)
        xf, xb = _ln_res(xf, mix, ln1_g[l][None], ln1_b[l][None], alpha, bm_ln, "ln1")

        j = l // 2
        if l % 2 == 0:
            xf, xb = _dense_ffn(xf, xb, ffn_w1[j].astype(BF16), ffn_w3[j].astype(BF16), ffn_w2[j].astype(BF16),
                                ln2_g[l][None], ln2_b[l][None], alpha, bm_mid, bm_ln)
        else:
            xf, xb = _moe(xf, moe_router[j], moe_router_b[j], moe_w1[j].astype(BF16),
                          moe_w3[j].astype(BF16), moe_w2[j].astype(BF16), ln2_g[l][None], ln2_b[l][None],
                          alpha, 512, bm_ln)

        kr_nat = _unspread(krf)
        new = (ckv[:tp].reshape(1, tp, -1), kr_nat[:tp].reshape(1, tp, -1),
               dk_new[:tp].reshape(1, tp, HEADS, HEAD_W), dv_new[:tp].reshape(1, tp, HEADS, HEAD_W),
               conv_p.reshape(1, CONV_W - 1, GROUP_W),
               ckv[tp:].reshape(nbs, ts, -1), kr_nat[tp:].reshape(nbs, ts, -1),
               dk_new[tp:].reshape(nbs, ts, HEADS, HEAD_W), dv_new[tp:].reshape(nbs, ts, HEADS, HEAD_W),
               conv_s, mv_s.reshape(nbs, ts, GROUP_W))
        for o, v in zip(outs, new):
            o.append(v)

    return (xf[:tp].reshape(1, tp, d), xf[tp:].reshape(nbs, ts, d)) + tuple(jnp.stack(o) for o in outs)
```

```python
import functools
import math

import numpy as np
import jax
import jax.numpy as jnp
from jax import lax
from jax.experimental import pallas as pl
from jax.experimental.pallas import tpu as pltpu

F32 = jnp.float32
BF16 = jnp.bfloat16

CHUNK_SHIFT = 6
GROUP_W = 1024
CONV_W = 3
HEADS = 8
MLA_NOPE = 128
MLA_ROPE = 64
MLA_KV_RANK = 512
HEAD_W = 128
DIFF_DIM = 64
MLP_CHUNK = 128
TOP_K = 2
ROPE_THETA = 10000.0
NEG_INF = -1e30
LOG2E = math.log2(math.e)
LANES = 128
VMEM_LIMIT = 56 * 1024 * 1024

C_AB, C_AC, C_AH, C_CQ = 0, 1024, 2048, 3072
C_DQ, C_DK, C_DV = 4096, 5120, 6144
C_DU, C_DVV = 7168, 8192
C_CKV = 9216
C_KR = 9728


def _params(sem, vmem=VMEM_LIMIT):
    return pltpu.CompilerParams(dimension_semantics=sem, vmem_limit_bytes=vmem)


def _pick(n, target, mult):
    best = None
    for d in range(mult, min(n, target) + 1, mult):
        if n % d == 0:
            best = d
    assert best is not None, (n, target, mult)
    return best


def _mm_kernel(x_ref, w_ref, o_ref):
    o_ref[...] = jnp.dot(x_ref[...], w_ref[...], preferred_element_type=F32).astype(o_ref.dtype)


def _mm(x, w, bm, bn, out_dtype, name):
    m, k = x.shape
    n = w.shape[1]
    return pl.pallas_call(
        _mm_kernel,
        grid=(m // bm, n // bn),
        in_specs=[pl.BlockSpec((bm, k), lambda i, j: (i, 0)),
                  pl.BlockSpec((k, bn), lambda i, j: (0, j))],
        out_specs=pl.BlockSpec((bm, bn), lambda i, j: (i, j)),
        out_shape=jax.ShapeDtypeStruct((m, n), out_dtype),
        compiler_params=_params(("parallel", "arbitrary")),
        name=name,
    )(x, w)


def _conv_prompt_kernel(b_ref, c_ref, h_ref, cp_ref, hp_ref, w_ref, y_ref, st_ref, zz_ref):
    i = pl.program_id(0)
    bm = b_ref.shape[0]
    z = c_ref[...] * h_ref[...]
    zprev = cp_ref[...] * hp_ref[...]
    zz_ref[0:8, :] = jnp.where(i == 0, 0.0, zprev)
    zz_ref[8:, :] = z
    w = w_ref[...]
    y = w[0:1] * zz_ref[pl.ds(6, bm), :] + w[1:2] * zz_ref[pl.ds(7, bm), :] + w[2:3] * z
    y_ref[...] = (b_ref[...] * y).astype(y_ref.dtype)
    st_ref[...] = z[bm - 8:, :]


def _conv_prompt(h, conv_w, tp, bm):
    nb = tp // bm
    r8 = bm // 8
    cw = GROUP_W
    y, st = pl.pallas_call(
        _conv_prompt_kernel,
        grid=(nb,),
        in_specs=[pl.BlockSpec((bm, cw), lambda i: (i, C_AB // cw)),
                  pl.BlockSpec((bm, cw), lambda i: (i, C_AC // cw)),
                  pl.BlockSpec((bm, cw), lambda i: (i, C_AH // cw)),
                  pl.BlockSpec((8, cw), lambda i: (jnp.maximum(i * r8 - 1, 0), C_AC // cw)),
                  pl.BlockSpec((8, cw), lambda i: (jnp.maximum(i * r8 - 1, 0), C_AH // cw)),
                  pl.BlockSpec((CONV_W, cw), lambda i: (0, 0))],
        out_specs=[pl.BlockSpec((bm, cw), lambda i: (i, 0)),
                   pl.BlockSpec((8, cw), lambda i: (0, 0))],
        out_shape=[jax.ShapeDtypeStruct((tp, cw), BF16),
                   jax.ShapeDtypeStruct((8, cw), F32)],
        scratch_shapes=[pltpu.VMEM((bm + 8, cw), F32)],
        compiler_params=_params(("arbitrary",)),
        name="conv_prompt",
    )(h, h, h, h, h, conv_w)
    return y, st[8 - (CONV_W - 1):]


def _conv_sample_kernel(b_ref, c_ref, h_ref, past_ref, w_ref, y_ref, st_ref, zz_ref):
    ts = b_ref.shape[0]
    z = c_ref[...] * h_ref[...]
    zz_ref[pl.ds(6, 2), :] = past_ref[0]
    zz_ref[pl.ds(8, ts), :] = z
    w = w_ref[...]
    y = w[0:1] * zz_ref[pl.ds(6, ts), :] + w[1:2] * zz_ref[pl.ds(7, ts), :] + w[2:3] * z
    y_ref[...] = (b_ref[...] * y).astype(y_ref.dtype)
    st_ref[0] = zz_ref[pl.ds(ts + 6, 2), :]


def _conv_sample(h, past, conv_w, tp, nb, ts):
    cw = GROUP_W
    r0 = tp // ts
    return pl.pallas_call(
        _conv_sample_kernel,
        grid=(nb,),
        in_specs=[pl.BlockSpec((ts, cw), lambda b: (r0 + b, C_AB // cw)),
                  pl.BlockSpec((ts, cw), lambda b: (r0 + b, C_AC // cw)),
                  pl.BlockSpec((ts, cw), lambda b: (r0 + b, C_AH // cw)),
                  pl.BlockSpec((1, CONV_W - 1, cw), lambda b: (b, 0, 0)),
                  pl.BlockSpec((CONV_W, cw), lambda b: (0, 0))],
        out_specs=[pl.BlockSpec((ts, cw), lambda b: (b, 0)),
                   pl.BlockSpec((1, CONV_W - 1, cw), lambda b: (b, 0, 0))],
        out_shape=[jax.ShapeDtypeStruct((nb * ts, cw), BF16),
                   jax.ShapeDtypeStruct((nb, CONV_W - 1, cw), F32)],
        scratch_shapes=[pltpu.VMEM((ts + 8, cw), F32)],
        compiler_params=_params(("parallel",)),
        name="conv_sample",
    )(h, h, h, past, conv_w)


def _rope_spread(r, cos, sin):
    return r * cos + pltpu.roll(r, 64, 1) * sin


def _mla_q_kernel(x_ref, g_ref, w_ref, cos_ref, sin_ref, o_ref, *, scale):
    x = x_ref[...]
    cn = x * lax.rsqrt(jnp.mean(x * x, axis=-1, keepdims=True) + 1e-6) * g_ref[...]
    q = jnp.dot(cn.astype(BF16), w_ref[...], preferred_element_type=F32)
    cos = cos_ref[...]
    sin = sin_ref[...]
    for hd in range(HEADS):
        c0 = hd * 2 * LANES
        o_ref[:, c0:c0 + LANES] = (q[:, c0:c0 + LANES] * scale).astype(o_ref.dtype)
        r = _rope_spread(q[:, c0 + LANES:c0 + 2 * LANES], cos, sin)
        o_ref[:, c0 + LANES:c0 + 2 * LANES] = (r * scale).astype(o_ref.dtype)


def _mla_q(h, g, w, cos, sin, bm):
    t = h.shape[0]
    n = w.shape[1]
    scale = (MLA_NOPE + MLA_ROPE) ** -0.5 * LOG2E
    return pl.pallas_call(
        functools.partial(_mla_q_kernel, scale=scale),
        grid=(t // bm,),
        in_specs=[pl.BlockSpec((bm, GROUP_W), lambda i: (i, C_CQ // GROUP_W)),
                  pl.BlockSpec((1, GROUP_W), lambda i: (0, 0)),
                  pl.BlockSpec(w.shape, lambda i: (0, 0)),
                  pl.BlockSpec((bm, LANES), lambda i: (i, 0)),
                  pl.BlockSpec((bm, LANES), lambda i: (i, 0))],
        out_specs=pl.BlockSpec((bm, n), lambda i: (i, 0)),
        out_shape=jax.ShapeDtypeStruct((t, n), BF16),
        compiler_params=_params(("parallel",)),
        name="mla_q",
    )(h, g, w, cos, sin)


def _mla_kv_kernel(x_ref, kr_ref, g_ref, w_ref, cos_ref, sin_ref, ckv_ref, krf_ref, kv_ref, krb_ref):
    x = x_ref[...]
    cn = x * lax.rsqrt(jnp.mean(x * x, axis=-1, keepdims=True) + 1e-6) * g_ref[...]
    ckv_ref[...] = cn
    kv_ref[...] = jnp.dot(cn.astype(BF16), w_ref[...], preferred_element_type=F32).astype(kv_ref.dtype)
    r = _rope_spread(kr_ref[...], cos_ref[...], sin_ref[...])
    krf_ref[...] = r
    krb_ref[...] = r.astype(krb_ref.dtype)


def _mla_kv(h, g, w, cos, sin, bm):
    t = h.shape[0]
    n = w.shape[1]
    return pl.pallas_call(
        _mla_kv_kernel,
        grid=(t // bm,),
        in_specs=[pl.BlockSpec((bm, MLA_KV_RANK), lambda i: (i, C_CKV // MLA_KV_RANK)),
                  pl.BlockSpec((bm, LANES), lambda i: (i, C_KR // LANES)),
                  pl.BlockSpec((1, MLA_KV_RANK), lambda i: (0, 0)),
                  pl.BlockSpec(w.shape, lambda i: (0, 0)),
                  pl.BlockSpec((bm, LANES), lambda i: (i, 0)),
                  pl.BlockSpec((bm, LANES), lambda i: (i, 0))],
        out_specs=[pl.BlockSpec((bm, MLA_KV_RANK), lambda i: (i, 0)),
                   pl.BlockSpec((bm, LANES), lambda i: (i, 0)),
                   pl.BlockSpec((bm, n), lambda i: (i, 0)),
                   pl.BlockSpec((bm, LANES), lambda i: (i, 0))],
        out_shape=[jax.ShapeDtypeStruct((t, MLA_KV_RANK), F32),
                   jax.ShapeDtypeStruct((t, LANES), F32),
                   jax.ShapeDtypeStruct((t, n), BF16),
                   jax.ShapeDtypeStruct((t, LANES), BF16)],
        compiler_params=_params(("parallel",)),
        name="mla_kv",
    )(h, h, g, w, cos, sin)


def _nt_dot(a, b):
    return lax.dot_general(a, b, (((1,), (1,)), ((), ())), preferred_element_type=F32)


def _chunk_mask(qpos, kpos):
    return (kpos >> CHUNK_SHIFT) <= (qpos >> CHUNK_SHIFT)


def _tri_pairs(tp, bq, bk):
    qi, kj, last = [], [], []
    for i in range(tp // bq):
        q_hi = (i + 1) * bq - 1
        k_end = ((q_hi >> CHUNK_SHIFT) + 1) << CHUNK_SHIFT
        nj = -(-min(k_end, tp) // bk)
        for j in range(nj):
            qi.append(i)
            kj.append(j)
            last.append(1 if j == nj - 1 else 0)
    return (jnp.asarray(qi, jnp.int32), jnp.asarray(kj, jnp.int32), jnp.asarray(last, jnp.int32))


def _softmax_step_t(s_t, v_t, m_ref, l_ref, acc_ref, i, r0):
    m_old = m_ref[i]
    m_new = jnp.maximum(m_old, jnp.max(s_t, axis=0, keepdims=True))
    alpha = jnp.exp2(m_old - m_new)
    e = jnp.exp2(s_t - m_new)
    l_ref[i] = alpha * l_ref[i] + jnp.sum(e, axis=0, keepdims=True)
    d = v_t.shape[0]
    acc_ref[r0:r0 + d, :] = (alpha * acc_ref[r0:r0 + d, :]
                             + jnp.dot(v_t, e.astype(BF16), preferred_element_type=F32))
    m_ref[i] = m_new


def _flash_mla_kernel(qi_ref, kj_ref, last_ref, q_ref, kn_ref, kr_ref, vt_ref, o_ref, m_ref, l_ref, acc_ref):
    p = pl.program_id(0)
    bq = q_ref.shape[0]
    bk = kn_ref.shape[0]
    j = kj_ref[p]

    @pl.when(j == 0)
    def _():
        m_ref[...] = jnp.full(m_ref.shape, NEG_INF, F32)
        l_ref[...] = jnp.zeros(l_ref.shape, F32)
        acc_ref[...] = jnp.zeros(acc_ref.shape, F32)

    kpos = j * bk + lax.broadcasted_iota(jnp.int32, (bk, 1), 0)
    qpos = qi_ref[p] * bq + lax.broadcasted_iota(jnp.int32, (1, bq), 1)
    bias = jnp.where(_chunk_mask(qpos, kpos), 0.0, NEG_INF)
    kr = kr_ref[...]
    for hd in range(HEADS):
        c0 = hd * HEAD_W
        k = jnp.concatenate([kn_ref[:, c0:c0 + HEAD_W], kr], axis=1)
        s_t = _nt_dot(k, q_ref[:, 2 * c0:2 * c0 + 2 * LANES]) + bias
        _softmax_step_t(s_t, vt_ref[c0:c0 + HEAD_W, :], m_ref, l_ref, acc_ref, hd, c0)

    @pl.when(last_ref[p] == 1)
    def _():
        for hd in range(HEADS):
            c0 = hd * HEAD_W
            o_t = acc_ref[c0:c0 + HEAD_W, :] / l_ref[hd]
            o_ref[:, c0:c0 + HEAD_W] = jnp.transpose(o_t).astype(o_ref.dtype)


def _flash_mla(q, kv, krb, vt, tp, bq, bk):
    qi, kj, last = _tri_pairs(tp, bq, bk)
    hw = HEADS * HEAD_W
    grid_spec = pltpu.PrefetchScalarGridSpec(
        num_scalar_prefetch=3,
        grid=(qi.shape[0],),
        in_specs=[pl.BlockSpec((bq, 2 * hw), lambda p, qi, kj, la: (qi[p], 0)),
                  pl.BlockSpec((bk, hw), lambda p, qi, kj, la: (kj[p], 0)),
                  pl.BlockSpec((bk, LANES), lambda p, qi, kj, la: (kj[p], 0)),
                  pl.BlockSpec((hw, bk), lambda p, qi, kj, la: (0, kj[p]))],
        out_specs=pl.BlockSpec((bq, hw), lambda p, qi, kj, la: (qi[p], 0)),
        scratch_shapes=[pltpu.VMEM((HEADS, 1, bq), F32), pltpu.VMEM((HEADS, 1, bq), F32),
                        pltpu.VMEM((hw, bq), F32)],
    )
    return pl.pallas_call(
        _flash_mla_kernel,
        grid_spec=grid_spec,
        out_shape=jax.ShapeDtypeStruct((tp, hw), BF16),
        compiler_params=_params(("arbitrary",)),
        name="flash_mla",
    )(qi, kj, last, q, kv, krb, vt)


def _samp_mla_kernel(q_ref, kn_ref, kr_ref, v_ref, o_ref, *, past):
    ts = q_ref.shape[0]
    tk = kn_ref.shape[1]
    k = jnp.concatenate([kn_ref[0], kr_ref[0]], axis=1)
    s = _nt_dot(q_ref[...], k)
    qpos = past + lax.broadcasted_iota(jnp.int32, (ts, 1), 0)
    kpos = lax.broadcasted_iota(jnp.int32, (1, tk), 1)
    s = jnp.where(_chunk_mask(qpos, kpos) & (kpos < past + ts), s, NEG_INF)
    e = jnp.exp2(s - jnp.max(s, axis=1, keepdims=True))
    pv = jnp.dot(e.astype(BF16), v_ref[0], preferred_element_type=F32)
    o_ref[...] = (pv / jnp.sum(e, axis=1, keepdims=True)).astype(o_ref.dtype)


def _samp_mla(q, kvfull, krfull, tp, nb, ts, past):
    r0 = tp // ts
    tk = kvfull.shape[1]
    return pl.pallas_call(
        functools.partial(_samp_mla_kernel, past=past),
        grid=(nb, HEADS),
        in_specs=[pl.BlockSpec((ts, 2 * LANES), lambda b, h: (r0 + b, h)),
                  pl.BlockSpec((1, tk, LANES), lambda b, h: (b, 0, h)),
                  pl.BlockSpec((1, tk, LANES), lambda b, h: (b, 0, 0)),
                  pl.BlockSpec((1, tk, LANES), lambda b, h: (b, 0, HEADS + h))],
        out_specs=pl.BlockSpec((ts, LANES), lambda b, h: (b, h)),
        out_shape=jax.ShapeDtypeStruct((nb * ts, HEADS * LANES), BF16),
        compiler_params=_params(("parallel", "parallel")),
        name="samp_mla",
    )(q, kvfull, krfull, kvfull)


def _split_q(q, scale):
    lane = lax.broadcasted_iota(jnp.int32, q.shape, 1)
    qs = q * scale
    qa = jnp.where(lane < DIFF_DIM, qs, 0.0).astype(BF16)
    qb = jnp.where(lane >= DIFF_DIM, qs, 0.0).astype(BF16)
    return qa, qb


def _subln(o, g, lambda_init):
    o = o * lax.rsqrt(jnp.mean(o * o, axis=-1, keepdims=True) + 1e-5) * g
    return o * (1.0 - lambda_init)


def _flash_diff_kernel(qi_ref, kj_ref, last_ref, slope_ref, lam_ref, q_ref, k_ref, vt_ref, g_ref, o_ref,
                       qs_ref, m_ref, l_ref, acc_ref, *, lambda_init):
    p = pl.program_id(0)
    bq = q_ref.shape[0]
    bk = k_ref.shape[0]
    hw = HEADS * HEAD_W
    j = kj_ref[p]

    @pl.when(j == 0)
    def _():
        m_ref[...] = jnp.full(m_ref.shape, NEG_INF, F32)
        l_ref[...] = jnp.zeros(l_ref.shape, F32)
        acc_ref[...] = jnp.zeros(acc_ref.shape, F32)
        for hd in range(HEADS):
            c0 = hd * HEAD_W
            qa, qb = _split_q(q_ref[:, c0:c0 + HEAD_W], DIFF_DIM ** -0.5 * LOG2E)
            qs_ref[:, c0:c0 + HEAD_W] = qa
            qs_ref[:, hw + c0:hw + c0 + HEAD_W] = qb

    kpos = j * bk + lax.broadcasted_iota(jnp.int32, (bk, 1), 0)
    qpos = qi_ref[p] * bq + lax.broadcasted_iota(jnp.int32, (1, bq), 1)
    ndist = jnp.where(_chunk_mask(qpos, kpos), -LOG2E * jnp.abs(qpos - kpos).astype(F32), NEG_INF)
    for hd in range(HEADS):
        c0 = hd * HEAD_W
        k = k_ref[:, c0:c0 + HEAD_W]
        v_t = vt_ref[c0:c0 + HEAD_W, :]
        bias = slope_ref[hd] * ndist
        for mp in range(2):
            r0 = mp * hw + c0
            s_t = _nt_dot(k, qs_ref[:, r0:r0 + HEAD_W]) + bias
            _softmax_step_t(s_t, v_t, m_ref, l_ref, acc_ref, mp * HEADS + hd, r0)

    @pl.when(last_ref[p] == 1)
    def _():
        for hd in range(HEADS):
            c0 = hd * HEAD_W
            o_t = (acc_ref[c0:c0 + HEAD_W, :] / l_ref[hd]
                   - lam_ref[0] * (acc_ref[hw + c0:hw + c0 + HEAD_W, :] / l_ref[HEADS + hd]))
            o_t = o_t * lax.rsqrt(jnp.mean(o_t * o_t, axis=0, keepdims=True) + 1e-5)
            o = jnp.transpose(o_t) * g_ref[...] * (1.0 - lambda_init)
            o_ref[:, c0:c0 + HEAD_W] = o.astype(o_ref.dtype)


def _flash_diff(h, kb, vt, slopes, lam, g, lambda_init, tp, bq, bk):
    qi, kj, last = _tri_pairs(tp, bq, bk)
    hw = HEADS * HEAD_W
    smem = pl.BlockSpec(memory_space=pltpu.SMEM)
    grid_spec = pltpu.PrefetchScalarGridSpec(
        num_scalar_prefetch=3,
        grid=(qi.shape[0],),
        in_specs=[smem, smem,
                  pl.BlockSpec((bq, hw), lambda p, qi, kj, la: (qi[p], C_DQ // hw)),
                  pl.BlockSpec((bk, hw), lambda p, qi, kj, la: (kj[p], 0)),
                  pl.BlockSpec((hw, bk), lambda p, qi, kj, la: (0, kj[p])),
                  pl.BlockSpec((1, HEAD_W), lambda p, qi, kj, la: (0, 0))],
        out_specs=pl.BlockSpec((bq, hw), lambda p, qi, kj, la: (qi[p], 0)),
        scratch_shapes=[pltpu.VMEM((bq, 2 * hw), BF16),
                        pltpu.VMEM((2 * HEADS, 1, bq), F32), pltpu.VMEM((2 * HEADS, 1, bq), F32),
                        pltpu.VMEM((2 * hw, bq), F32)],
    )
    return pl.pallas_call(
        functools.partial(_flash_diff_kernel, lambda_init=lambda_init),
        grid_spec=grid_spec,
        out_shape=jax.ShapeDtypeStruct((tp, hw), BF16),
        compiler_params=_params(("arbitrary",)),
        name="flash_diff",
    )(qi, kj, last, slopes, lam, h, kb, vt, g)


def _samp_diff_kernel(slope_ref, lam_ref, q_ref, k_ref, v_ref, g_ref, o_ref, *, past, lambda_init):
    hd = pl.program_id(1)
    ts = q_ref.shape[0]
    tk = k_ref.shape[1]
    qa, qb = _split_q(q_ref[...], DIFF_DIM ** -0.5)
    k = k_ref[0]
    v = v_ref[0]
    qpos = past + lax.broadcasted_iota(jnp.int32, (ts, 1), 0)
    kpos = lax.broadcasted_iota(jnp.int32, (1, tk), 1)
    dist = jnp.abs(qpos - kpos).astype(F32)
    ok = _chunk_mask(qpos, kpos) & (kpos < past + ts)
    bias = jnp.where(ok, -slope_ref[hd] * dist, NEG_INF)
    outs = []
    for qx in (qa, qb):
        s = _nt_dot(qx, k) + bias
        e = jnp.exp(s - jnp.max(s, axis=1, keepdims=True))
        pv = jnp.dot(e.astype(BF16), v, preferred_element_type=F32)
        outs.append(pv / jnp.sum(e, axis=1, keepdims=True))
    o = outs[0] - lam_ref[0] * outs[1]
    o_ref[...] = _subln(o, g_ref[...], lambda_init).astype(o_ref.dtype)


def _samp_diff(h, kfull, vfull, slopes, lam, g, lambda_init, tp, nb, ts, past):
    r0 = tp // ts
    tk = kfull.shape[1]
    smem = pl.BlockSpec(memory_space=pltpu.SMEM)
    return pl.pallas_call(
        functools.partial(_samp_diff_kernel, past=past, lambda_init=lambda_init),
        grid=(nb, HEADS),
        in_specs=[smem, smem,
                  pl.BlockSpec((ts, HEAD_W), lambda b, h: (r0 + b, C_DQ // HEAD_W + h)),
                  pl.BlockSpec((1, tk, HEAD_W), lambda b, h: (b, 0, h)),
                  pl.BlockSpec((1, tk, HEAD_W), lambda b, h: (b, 0, h)),
                  pl.BlockSpec((1, HEAD_W), lambda b, h: (0, 0))],
        out_specs=pl.BlockSpec((ts, HEAD_W), lambda b, h: (b, h)),
        out_shape=jax.ShapeDtypeStruct((nb * ts, HEADS * HEAD_W), BF16),
        compiler_params=_params(("parallel", "parallel")),
        name="samp_diff",
    )(slopes, lam, h, kfull, vfull, g)


def _layernorm_rows(z, g, b):
    mu = jnp.mean(z, axis=-1, keepdims=True)
    var = jnp.mean(jnp.square(z - mu), axis=-1, keepdims=True)
    return (z - mu) * lax.rsqrt(var + 1e-5) * g + b


def _gelu(x):
    return 0.5 * x * (1.0 + jnp.tanh(math.sqrt(2.0 / math.pi) * (x + 0.044715 * (x * x * x))))


def _chunk_mlp_kernel(u_ref, v_ref, g_ref, b_ref, w_ref, bs_ref, y_ref, *mv_ref):
    u = _gelu(u_ref[...])
    v = _layernorm_rows(_gelu(v_ref[...]), g_ref[...], b_ref[...])
    if mv_ref:
        mv_ref[0][...] = v
    vb = v.astype(BF16)
    bs = bs_ref[...]
    for gi in range(GROUP_W // LANES):
        c0 = gi * LANES
        s = jnp.dot(w_ref[gi], vb[:, c0:c0 + LANES], preferred_element_type=F32) + bs[:, gi:gi + 1]
        y_ref[:, c0:c0 + LANES] = (u[:, c0:c0 + LANES] * s).astype(y_ref.dtype)


def _chunk_mlp(h, g, b, w_tril, bs_t, row0, nblk, ln, with_mv, name):
    r0 = row0 // ln
    cw = GROUP_W
    n_out = 2 if with_mv else 1
    return pl.pallas_call(
        _chunk_mlp_kernel,
        grid=(nblk,),
        in_specs=[pl.BlockSpec((ln, cw), lambda i: (r0 + i, C_DU // cw)),
                  pl.BlockSpec((ln, cw), lambda i: (r0 + i, C_DVV // cw)),
                  pl.BlockSpec((1, cw), lambda i: (0, 0)),
                  pl.BlockSpec((1, cw), lambda i: (0, 0)),
                  pl.BlockSpec(w_tril.shape, lambda i: (0, 0, 0)),
                  pl.BlockSpec(bs_t.shape, lambda i: (0, 0))],
        out_specs=[pl.BlockSpec((ln, cw), lambda i: (i, 0)),
                   pl.BlockSpec((ln, cw), lambda i: (i, 0))][:n_out],
        out_shape=[jax.ShapeDtypeStruct((nblk * ln, cw), BF16),
                   jax.ShapeDtypeStruct((nblk * ln, cw), F32)][:n_out],
        compiler_params=_params(("parallel",)),
        name=name,
    )(h, h, g, b, w_tril, bs_t)


def _mm4_kernel(a_ref, b_ref, c_ref, d_ref, w_ref, o_ref):
    gw = a_ref.shape[1]
    acc = jnp.dot(a_ref[...], w_ref[0:gw, :], preferred_element_type=F32)
    acc += jnp.dot(b_ref[...], w_ref[gw:2 * gw, :], preferred_element_type=F32)
    acc += jnp.dot(c_ref[...], w_ref[2 * gw:3 * gw, :], preferred_element_type=F32)
    acc += jnp.dot(d_ref[...], w_ref[3 * gw:4 * gw, :], preferred_element_type=F32)
    o_ref[...] = acc


def _out_proj(ya, yb, yc, yd, w, bm, bn):
    t, gw = ya.shape
    n = w.shape[1]
    xs = pl.BlockSpec((bm, gw), lambda i, j: (i, 0))
    return pl.pallas_call(
        _mm4_kernel,
        grid=(t // bm, n // bn),
        in_specs=[xs, xs, xs, xs, pl.BlockSpec((4 * gw, bn), lambda i, j: (0, j))],
        out_specs=pl.BlockSpec((bm, bn), lambda i, j: (i, j)),
        out_shape=jax.ShapeDtypeStruct((t, n), F32),
        compiler_params=_params(("parallel", "arbitrary")),
        name="out_proj",
    )(ya, yb, yc, yd, w)


def _ln_res_kernel(x_ref, y_ref, g_ref, b_ref, o_ref, ob_ref, *, alpha):
    out = _layernorm_rows(alpha * x_ref[...] + y_ref[...], g_ref[...], b_ref[...])
    o_ref[...] = out
    ob_ref[...] = out.astype(ob_ref.dtype)


def _ln_res(x, y, g, b, alpha, bm, name):
    t, d = x.shape
    return pl.pallas_call(
        functools.partial(_ln_res_kernel, alpha=alpha),
        grid=(t // bm,),
        in_specs=[pl.BlockSpec((bm, d), lambda i: (i, 0)),
                  pl.BlockSpec((bm, d), lambda i: (i, 0)),
                  pl.BlockSpec((1, d), lambda i: (0, 0)),
                  pl.BlockSpec((1, d), lambda i: (0, 0))],
        out_specs=[pl.BlockSpec((bm, d), lambda i: (i, 0)),
                   pl.BlockSpec((bm, d), lambda i: (i, 0))],
        out_shape=[jax.ShapeDtypeStruct((t, d), F32), jax.ShapeDtypeStruct((t, d), BF16)],
        compiler_params=_params(("parallel",)),
        name=name,
    )(x, y, g, b)


def _ffn_up_kernel(be_ref, nu_ref, x_ref, w1_ref, w3_ref, o_ref):
    used = pl.program_id(1) < nu_ref[0]

    @pl.when(used)
    def _():
        x = x_ref[...]
        a = jnp.dot(x, w1_ref[0], preferred_element_type=F32)
        b = jnp.dot(x, w3_ref[0], preferred_element_type=F32)
        o_ref[...] = (a * (1.0 / (1.0 + jnp.exp(-a))) * b).astype(o_ref.dtype)

    @pl.when(jnp.logical_not(used))
    def _():
        o_ref[...] = jnp.zeros(o_ref.shape, o_ref.dtype)


def _ffn_up(x, w1, w3, blk_e, n_used, bm, bn):
    m, k = x.shape
    n = w1.shape[2]

    def xmap(j, i, be, nu):
        return (jnp.minimum(i, nu[0] - 1), 0)

    def wmap(j, i, be, nu):
        return (be[jnp.minimum(i, nu[0] - 1)], 0, j)

    def omap(j, i, be, nu):
        return (i, j)

    grid_spec = pltpu.PrefetchScalarGridSpec(
        num_scalar_prefetch=2,
        grid=(n // bn, m // bm),
        in_specs=[pl.BlockSpec((bm, k), xmap),
                  pl.BlockSpec((1, k, bn), wmap),
                  pl.BlockSpec((1, k, bn), wmap)],
        out_specs=pl.BlockSpec((bm, bn), omap),
    )
    return pl.pallas_call(
        _ffn_up_kernel,
        grid_spec=grid_spec,
        out_shape=jax.ShapeDtypeStruct((m, n), BF16),
        compiler_params=_params(("parallel", "arbitrary")),
        name="ffn_up",
    )(blk_e, n_used, x, w1, w3)


def _ffn_down_kernel(be_ref, nu_ref, h_ref, w_ref, o_ref):
    kk = pl.program_id(1)
    part = jnp.dot(h_ref[...], w_ref[0], preferred_element_type=F32)

    @pl.when(kk == 0)
    def _():
        o_ref[...] = part

    @pl.when(kk > 0)
    def _():
        o_ref[...] += part


def _ffn_down_slab_kernel(be_ref, nu_ref, h_ref, w_ref, g_ref, o_ref, acc_ref):
    kk = pl.program_id(1)
    bm = acc_ref.shape[0]
    used = pl.program_id(0) < nu_ref[0]
    final = kk == pl.num_programs(1) - 1

    @pl.when(used)
    def _():
        part = jnp.dot(h_ref[...], w_ref[0], preferred_element_type=F32)

        @pl.when(kk == 0)
        def _():
            acc_ref[...] = part

        @pl.when(kk > 0)
        def _():
            acc_ref[...] += part

    @pl.when(used & final)
    def _():
        acc_ref[...] = acc_ref[...] * g_ref[...]

        def row(r, c):
            o_ref[r] = acc_ref[pl.ds(r, 1), :]
            return c

        lax.fori_loop(0, bm, row, 0, unroll=8)

    @pl.when(jnp.logical_not(used) & final)
    def _():
        o_ref[...] = jnp.zeros(o_ref.shape, o_ref.dtype)


def _ffn_down(hm, w2, gate, blk_e, n_used, bm, bk):
    m, kdim = hm.shape
    n = w2.shape[2]
    nk = kdim // bk

    def hmap(i, kk, be, nu):
        return (jnp.minimum(i, nu[0] - 1), jnp.where(i < nu[0], kk, nk - 1))

    def wmap(i, kk, be, nu):
        return (be[jnp.minimum(i, nu[0] - 1)], jnp.where(i < nu[0], kk, nk - 1), 0)

    def gmap(i, kk, be, nu):
        return (jnp.minimum(i, nu[0] - 1), 0)

    in_specs = [pl.BlockSpec((bm, bk), hmap), pl.BlockSpec((1, bk, n), wmap)]
    if gate is None:
        body, args, scratch = _ffn_down_kernel, (hm, w2), []
        out_spec = pl.BlockSpec((bm, n), lambda i, kk, be, nu: (i, 0))
        out_shape = jax.ShapeDtypeStruct((m, n), F32)
    else:
        body, args, scratch = _ffn_down_slab_kernel, (hm, w2, gate), [pltpu.VMEM((bm, n), F32)]
        in_specs.append(pl.BlockSpec((bm, 1), gmap))
        out_spec = pl.BlockSpec((bm, 1, n), lambda i, kk, be, nu: (i, 0, 0))
        out_shape = jax.ShapeDtypeStruct((m, 1, n), F32)
    grid_spec = pltpu.PrefetchScalarGridSpec(
        num_scalar_prefetch=2, grid=(m // bm, nk), in_specs=in_specs, out_specs=out_spec,
        scratch_shapes=scratch)
    return pl.pallas_call(
        body,
        grid_spec=grid_spec,
        out_shape=out_shape,
        compiler_params=_params(("arbitrary", "arbitrary")),
        name="ffn_down",
    )(blk_e, n_used, *args)


def _router_kernel(x_ref, w_ref, b_ref, idx_ref, gate_ref, cnt_ref, base_ref, *, n_experts):
    @pl.when(pl.program_id(0) == 0)
    def _():
        base_ref[...] = jnp.zeros(base_ref.shape, F32)

    bm = x_ref.shape[0]
    logits = jnp.dot(x_ref[...], w_ref[...], preferred_element_type=F32,
                     precision=lax.Precision.HIGHEST) + b_ref[...]
    lane = lax.broadcasted_iota(jnp.int32, logits.shape, 1)
    logits = jnp.where(lane < n_experts, logits, -jnp.inf)
    v1 = jnp.max(logits, axis=1, keepdims=True)
    i1 = jnp.min(jnp.where(logits == v1, lane, LANES), axis=1, keepdims=True)
    rest = jnp.where(lane == i1, -jnp.inf, logits)
    v2 = jnp.max(rest, axis=1, keepdims=True)
    i2 = jnp.min(jnp.where(rest == v2, lane, LANES), axis=1, keepdims=True)
    e2 = jnp.exp(v2 - v1)
    g1 = 1.0 / (1.0 + e2)
    g2 = e2 / (1.0 + e2)
    oh1 = (lane == i1).astype(F32)
    oh2 = (lane == i2).astype(F32)
    both = oh1 + oh2
    row = lax.broadcasted_iota(jnp.int32, (bm, bm), 0)
    col = lax.broadcasted_iota(jnp.int32, (bm, bm), 1)
    tril = (col <= row).astype(BF16)
    incl = jnp.dot(tril, both.astype(BF16), preferred_element_type=F32)
    before = base_ref[...] + incl - both
    r1 = jnp.sum(oh1 * before, axis=1, keepdims=True).astype(jnp.int32)
    r2 = jnp.sum(oh2 * before, axis=1, keepdims=True).astype(jnp.int32)
    total = base_ref[...] + incl[bm - 1:bm, :]
    base_ref[...] = total
    cnt_ref[...] = jnp.broadcast_to(total, cnt_ref.shape)
    idx_ref[...] = jnp.where(lane == 0, i1, jnp.where(lane == 1, i2,
                             jnp.where(lane == 2, r1, jnp.where(lane == 3, r2, 0))))
    gate_ref[...] = jnp.where(lane == 0, g1, jnp.where(lane == 1, g2, 0.0))


def _router(x, w_pad, b_pad, n_experts, bm):
    t, d = x.shape
    return pl.pallas_call(
        functools.partial(_router_kernel, n_experts=n_experts),
        grid=(t // bm,),
        in_specs=[pl.BlockSpec((bm, d), lambda i: (i, 0)),
                  pl.BlockSpec((d, LANES), lambda i: (0, 0)),
                  pl.BlockSpec((1, LANES), lambda i: (0, 0))],
        out_specs=[pl.BlockSpec((bm, LANES), lambda i: (i, 0)),
                   pl.BlockSpec((bm, LANES), lambda i: (i, 0)),
                   pl.BlockSpec((8, LANES), lambda i: (0, 0))],
        out_shape=[jax.ShapeDtypeStruct((t, LANES), jnp.int32), jax.ShapeDtypeStruct((t, LANES), F32),
                   jax.ShapeDtypeStruct((8, LANES), F32)],
        scratch_shapes=[pltpu.VMEM((1, LANES), F32)],
        compiler_params=_params(("arbitrary",)),
        name="router",
    )(x, w_pad, b_pad)


def _start_row_gathers(src_ref, idx_ref, first, buf_ref, sem, rows):
    def body(r, c):
        pltpu.make_async_copy(src_ref.at[idx_ref[first + r]], buf_ref.at[r], sem).start()
        return c

    lax.fori_loop(0, rows, body, 0, unroll=4)


def _wait_row_gathers(src_ref, buf_ref, sem, rows):
    def body(r, c):
        pltpu.make_async_copy(src_ref.at[0], buf_ref.at[r], sem).wait()
        return c

    lax.fori_loop(0, rows, body, 0, unroll=4)


def _gather_x_kernel(idx_ref, src_ref, o_ref, buf_ref, stage_ref, sem):
    i = pl.program_id(0)
    rows = stage_ref.shape[0]
    slot = i % 2

    @pl.when(i == 0)
    def _():
        _start_row_gathers(src_ref, idx_ref, 0, buf_ref.at[0], sem.at[0], rows)

    @pl.when(i + 1 < pl.num_programs(0))
    def _():
        _start_row_gathers(src_ref, idx_ref, (i + 1) * rows, buf_ref.at[1 - slot], sem.at[1 - slot], rows)

    _wait_row_gathers(src_ref, buf_ref.at[slot], sem.at[slot], rows)

    def row(r, c):
        stage_ref[pl.ds(r, 1), :] = buf_ref[slot, r]
        return c

    lax.fori_loop(0, rows, row, 0, unroll=8)
    o_ref[...] = stage_ref[...].astype(o_ref.dtype)


def _gather_x(src, idx, rows):
    n = idx.shape[0]
    width = src.shape[2]
    grid_spec = pltpu.PrefetchScalarGridSpec(
        num_scalar_prefetch=1,
        grid=(n // rows,),
        in_specs=[pl.BlockSpec(memory_space=pl.ANY)],
        out_specs=pl.BlockSpec((rows, width), lambda i, idx: (i, 0)),
        scratch_shapes=[pltpu.VMEM((2, rows, 1, width), F32), pltpu.VMEM((rows, width), F32),
                        pltpu.SemaphoreType.DMA((2,))],
    )
    return pl.pallas_call(
        _gather_x_kernel,
        grid_spec=grid_spec,
        out_shape=jax.ShapeDtypeStruct((n, width), BF16),
        compiler_params=_params(("arbitrary",)),
        name="gather_x",
    )(idx, src)


def _moe_combine_kernel(d0_ref, d1_ref, x_ref, y_ref, g_ref, b_ref, o_ref, ob_ref, buf_ref, stage_ref, sem,
                        *, alpha):
    i = pl.program_id(0)
    rows = stage_ref.shape[0]
    slot = i % 2

    def start(blk, s):
        _start_row_gathers(y_ref, d0_ref, blk * rows, buf_ref.at[s, 0], sem.at[s], rows)
        _start_row_gathers(y_ref, d1_ref, blk * rows, buf_ref.at[s, 1], sem.at[s], rows)

    @pl.when(i == 0)
    def _():
        start(0, 0)

    @pl.when(i + 1 < pl.num_programs(0))
    def _():
        start(i + 1, 1 - slot)

    _wait_row_gathers(y_ref, buf_ref.at[slot, 0], sem.at[slot], rows)
    _wait_row_gathers(y_ref, buf_ref.at[slot, 1], sem.at[slot], rows)

    def row(r, c):
        stage_ref[pl.ds(r, 1), :] = buf_ref[slot, 0, r] + buf_ref[slot, 1, r]
        return c

    lax.fori_loop(0, rows, row, 0, unroll=8)
    out = _layernorm_rows(alpha * x_ref[...] + stage_ref[...], g_ref[...], b_ref[...])
    o_ref[...] = out
    ob_ref[...] = out.astype(ob_ref.dtype)


def _moe_combine(x, y_slabs, d0, d1, g, b, alpha, rows):
    t, d = x.shape
    grid_spec = pltpu.PrefetchScalarGridSpec(
        num_scalar_prefetch=2,
        grid=(t // rows,),
        in_specs=[pl.BlockSpec((rows, d), lambda i, d0, d1: (i, 0)),
                  pl.BlockSpec(memory_space=pl.ANY),
                  pl.BlockSpec((1, d), lambda i, d0, d1: (0, 0)),
                  pl.BlockSpec((1, d), lambda i, d0, d1: (0, 0))],
        out_specs=[pl.BlockSpec((rows, d), lambda i, d0, d1: (i, 0)),
                   pl.BlockSpec((rows, d), lambda i, d0, d1: (i, 0))],
        scratch_shapes=[pltpu.VMEM((2, 2, rows, 1, d), F32), pltpu.VMEM((rows, d), F32),
                        pltpu.SemaphoreType.DMA((2,))],
    )
    return pl.pallas_call(
        functools.partial(_moe_combine_kernel, alpha=alpha),
        grid_spec=grid_spec,
        out_shape=[jax.ShapeDtypeStruct((t, d), F32), jax.ShapeDtypeStruct((t, d), BF16)],
        compiler_params=_params(("arbitrary",)),
        name="moe_combine",
    )(d0, d1, x, y_slabs, g, b)


def _moe(xf, w_router, b_router, w1, w3, w2, ln_g, ln_b, alpha, bm_e, bm_ln):
    t, d = xf.shape
    n_e = w1.shape[0]
    w_pad = jnp.zeros((d, LANES), F32).at[:, :n_e].set(w_router)
    b_pad = jnp.zeros((1, LANES), F32).at[0, :n_e].set(b_router)
    idx, gates, cnt = _router(xf, w_pad, b_pad, n_e, bm_ln)
    a = t * TOP_K
    flat_e = idx[:, :TOP_K].reshape(-1)
    rank = idx[:, TOP_K:2 * TOP_K].reshape(-1)
    flat_g = gates[:, :TOP_K].reshape(-1)
    flat_t = jnp.arange(a, dtype=jnp.int32) // TOP_K
    counts = cnt[0, :n_e].astype(jnp.int32)
    padded = (counts + bm_e - 1) // bm_e * bm_e
    pend = jnp.cumsum(padded)
    pstart = pend - padded
    dest = (jnp.take(pstart, flat_e) + rank).astype(jnp.int32)
    nb = -(-a // bm_e) + n_e
    tok = jnp.zeros(nb * bm_e, jnp.int32).at[dest].set(flat_t)
    gate = jnp.zeros(nb * bm_e, F32).at[dest].set(flat_g)
    blk_e = jnp.minimum(jnp.searchsorted(pend, jnp.arange(nb, dtype=jnp.int32) * bm_e, side='right'),
                        n_e - 1).astype(jnp.int32)
    n_used = (pend[-1] // bm_e).astype(jnp.int32).reshape(1)
    xg = _gather_x(xf.reshape(t, 1, d), tok, bm_e)
    hm = _ffn_up(xg, w1, w3, blk_e, n_used, bm_e, _pick(w1.shape[2], 1024, LANES))
    yb = _ffn_down(hm, w2, gate[:, None], blk_e, n_used, bm_e, _pick(w2.shape[1], 1024, LANES))
    dest2 = dest.reshape(t, TOP_K)
    return _moe_combine(xf, yb, dest2[:, 0], dest2[:, 1], ln_g, ln_b, alpha, bm_ln)


def _dense_ffn(xf, xb, w1, w3, w2, ln_g, ln_b, alpha, bm, bm_ln):
    t, d = xf.shape
    nblk = t // bm
    blk_e = jnp.zeros((nblk,), jnp.int32)
    n_used = jnp.full((1,), nblk, jnp.int32)
    hm = _ffn_up(xb, w1[None], w3[None], blk_e, n_used, bm, _pick(w1.shape[1], 1024, LANES))
    f = _ffn_down(hm, w2[None], None, blk_e, n_used, bm, _pick(w2.shape[0], 1024, LANES))
    return _ln_res(xf, f, ln_g, ln_b, alpha, bm_ln, "ln2_dense")


def _spread_rope_cols(w):
    z = jnp.zeros(w.shape[:-1] + (32,), w.dtype)
    return jnp.concatenate([w[..., :32], z, w[..., 32:], z], axis=-1)


def _prep_w_in(w):
    k0 = C_CQ + GROUP_W
    k1 = k0 + MLA_KV_RANK
    return jnp.concatenate([w[:, :k0], w[:, k1 + MLA_ROPE:], w[:, k0:k1],
                            _spread_rope_cols(w[:, k1:k1 + MLA_ROPE])], axis=1).astype(BF16)


def _prep_w_qb(w):
    r = w.shape[0]
    w = w.reshape(r, HEADS, MLA_NOPE + MLA_ROPE)
    out = jnp.concatenate([w[..., :MLA_NOPE], _spread_rope_cols(w[..., MLA_NOPE:])], axis=-1)
    return out.reshape(r, HEADS * 2 * LANES).astype(BF16)


def _prep_w_kvb(w):
    r = w.shape[0]
    w = w.reshape(r, HEADS, MLA_NOPE + HEAD_W)
    out = jnp.concatenate([w[..., :MLA_NOPE].reshape(r, -1), w[..., MLA_NOPE:].reshape(r, -1)], axis=-1)
    return out.astype(BF16)


def _rope_tables(pos):
    half = MLA_ROPE // 2
    freqs = ROPE_THETA ** (-jnp.arange(half, dtype=F32) / half)
    ang = pos.astype(F32)[:, None] * freqs
    cos, sin = jnp.cos(ang), jnp.sin(ang)
    z = jnp.zeros_like(cos)
    return (jnp.concatenate([cos, z, cos, z], axis=1), jnp.concatenate([-sin, z, sin, z], axis=1))


def _unspread(r):
    return jnp.concatenate([r[..., :32], r[..., 64:96]], axis=-1)


def _pad_keys(past, new, tk_pad):
    b, p, w = past.shape
    ts = new.shape[1]
    pad = jnp.zeros((b, tk_pad - p - ts, w), past.dtype)
    return jnp.concatenate([past, new, pad], axis=1)


def kernel(x_prompt, x_sample, cache_mla_ckv, cache_mla_krope, cache_diff_k, cache_diff_v, state_conv, w_in, conv_w, mla_q_norm, mla_w_qb, mla_kv_norm, mla_w_kvb, diff_lq1, diff_lk1, diff_lq2, diff_lk2, diff_subln, mlp_v_norm_g, mlp_v_norm_b, mlp_ws, mlp_bs, w_out, ln1_g, ln1_b, ln2_g, ln2_b, ffn_w1, ffn_w3, ffn_w2, moe_router, moe_router_b, moe_w1, moe_w3, moe_w2):
    nbp, tp, d = x_prompt.shape
    nbs, ts, _ = x_sample.shape
    depth = w_in.shape[0]
    past = cache_mla_ckv.shape[2]
    assert nbp == 1 and tp % MLP_CHUNK == 0 and ts % 16 == 0 and ts < MLP_CHUNK and tp % ts == 0
    t = tp + nbs * ts
    alpha = (2 * depth) ** 0.25
    tk_pad = -(-(past + ts) // LANES) * LANES

    bm_big = _pick(t, 1280, 16)
    bm_mid = _pick(t, 640, 16)
    bm_ln = _pick(t, 256, 16)
    bq = _pick(tp, 512, 64)
    bm_conv = _pick(tp, 256, 8)

    pos = jnp.concatenate([jnp.arange(tp, dtype=jnp.int32),
                           jnp.tile(past + jnp.arange(ts, dtype=jnp.int32), nbs)])
    cos, sin = _rope_tables(pos)
    slopes = 2.0 ** (-8.0 * jnp.arange(1, HEADS + 1, dtype=F32) / HEADS)

    xf = jnp.concatenate([x_prompt.reshape(tp, d), x_sample.reshape(nbs * ts, d)], axis=0)
    xb = xf.astype(BF16)
    outs = [[] for _ in range(11)]
    for l in range(depth):
        lambda_init = 0.8 - 0.6 * math.exp(-0.3 * l)
        h = _mm(xb, _prep_w_in(w_in[l]), bm_big, 896, F32, "in_proj")

        ya_p, conv_p = _conv_prompt(h, conv_w[l], tp, bm_conv)
        ya_s, conv_s = _conv_sample(h, state_conv[l], conv_w[l], tp, nbs, ts)

        q = _mla_q(h, mla_q_norm[l][None], _prep_w_qb(mla_w_qb[l]), cos, sin, bm_mid)
        w_kvb = _prep_w_kvb(mla_w_kvb[l])
        ckv, krf, kv, krb = _mla_kv(h, mla_kv_norm[l][None], w_kvb, cos, sin, bm_mid)
        yb_p = _flash_mla(q, kv, krb, jnp.transpose(kv[:tp, HEADS * HEAD_W:]), tp, bq, bq)
        ckv_cache = cache_mla_ckv[l].reshape(nbs * past, MLA_KV_RANK).astype(BF16)
        kv_cache = _mm(ckv_cache, w_kvb, _pick(nbs * past, 1024, 16), w_kvb.shape[1], BF16, "kv_cache")
        kv_full = _pad_keys(kv_cache.reshape(nbs, past, -1), kv[tp:].reshape(nbs, ts, -1), tk_pad)
        kr_full = _pad_keys(_spread_rope_cols(cache_mla_krope[l]).astype(BF16),
                            krb[tp:].reshape(nbs, ts, LANES), tk_pad)
        yb_s = _samp_mla(q, kv_full, kr_full, tp, nbs, ts, past)

        lam = (jnp.exp(jnp.sum(diff_lq1[l] * diff_lk1[l])) - jnp.exp(jnp.sum(diff_lq2[l] * diff_lk2[l]))
               + lambda_init).reshape(1).astype(F32)
        sub_g = diff_subln[l][None]
        dk_new = h[:, C_DK:C_DK + GROUP_W]
        dv_new = h[:, C_DV:C_DV + GROUP_W]
        yc_p = _flash_diff(h, dk_new[:tp].astype(BF16), jnp.transpose(dv_new[:tp]).astype(BF16),
                           slopes, lam, sub_g, lambda_init, tp, bq, bq)
        dk_full = _pad_keys(cache_diff_k[l].reshape(nbs, past, GROUP_W).astype(BF16),
                            dk_new[tp:].reshape(nbs, ts, GROUP_W).astype(BF16), tk_pad)
        dv_full = _pad_keys(cache_diff_v[l].reshape(nbs, past, GROUP_W).astype(BF16),
                            dv_new[tp:].reshape(nbs, ts, GROUP_W).astype(BF16), tk_pad)
        yc_s = _samp_diff(h, dk_full, dv_full, slopes, lam, sub_g, lambda_init, tp, nbs, ts, past)

        tril = jnp.tril(jnp.ones((MLP_CHUNK, MLP_CHUNK), F32))
        w_sp = (mlp_ws[l] * tril).astype(BF16)
        mg, mb = mlp_v_norm_g[l][None], mlp_v_norm_b[l][None]
        yd_p, = _chunk_mlp(h, mg, mb, w_sp, jnp.transpose(mlp_bs[l]), 0, tp // MLP_CHUNK, MLP_CHUNK, False,
                           "chunk_mlp_prompt")
        yd_s, mv_s = _chunk_mlp(h, mg, mb, w_sp[:, :ts, :ts], jnp.transpose(mlp_bs[l][:, :ts]), tp, nbs, ts,
                                True, "chunk_mlp_sample")

        ya = jnp.concatenate([ya_p, ya_s], axis=0)
        yb = jnp.concatenate([yb_p, yb_s], axis=0)
        yc = jnp.concatenate([yc_p, yc_s], axis=0)
        yd = jnp.concatenate([yd_p, yd_s], axis=0)
        mix = _out_proj(ya, yb, yc, yd, w_out[l].astype(BF16), bm_big, 1024)
        xf, xb = _ln_res(xf, mix, ln1_g[l][None], ln1_b[l][None], alpha, bm_ln, "ln1")

        j = l // 2
        if l % 2 == 0:
            xf, xb = _dense_ffn(xf, xb, ffn_w1[j].astype(BF16), ffn_w3[j].astype(BF16), ffn_w2[j].astype(BF16),
                                ln2_g[l][None], ln2_b[l][None], alpha, bm_mid, bm_ln)
        else:
            xf, xb = _moe(xf, moe_router[j], moe_router_b[j], moe_w1[j].astype(BF16),
                          moe_w3[j].astype(BF16), moe_w2[j].astype(BF16), ln2_g[l][None], ln2_b[l][None],
                          alpha, 512, bm_ln)

        kr_nat = _unspread(krf)
        new = (ckv[:tp].reshape(1, tp, -1), kr_nat[:tp].reshape(1, tp, -1),
               dk_new[:tp].reshape(1, tp, HEADS, HEAD_W), dv_new[:tp].reshape(1, tp, HEADS, HEAD_W),
               conv_p.reshape(1, CONV_W - 1, GROUP_W),
               ckv[tp:].reshape(nbs, ts, -1), kr_nat[tp:].reshape(nbs, ts, -1),
               dk_new[tp:].reshape(nbs, ts, HEADS, HEAD_W), dv_new[tp:].reshape(nbs, ts, HEADS, HEAD_W),
               conv_s, mv_s.reshape(nbs, ts, GROUP_W))
        for o, v in zip(outs, new):
            o.append(v)

    return (xf[:tp].reshape(1, tp, d), xf[tp:].reshape(nbs, ts, d)) + tuple(jnp.stack(o) for o in outs)
```

```python
import functools
import math

import numpy as np
import jax
import jax.numpy as jnp
from jax import lax
from jax.experimental import pallas as pl
from jax.experimental.pallas import tpu as pltpu

F32 = jnp.float32
BF16 = jnp.bfloat16

CHUNK_SHIFT = 6
GROUP_W = 1024
CONV_W = 3
HEADS = 8
MLA_NOPE = 128
MLA_ROPE = 64
MLA_KV_RANK = 512
HEAD_W = 128
DIFF_DIM = 64
MLP_CHUNK = 128
TOP_K = 2
ROPE_THETA = 10000.0
NEG_INF = -1e30
LOG2E = math.log2(math.e)
LANES = 128
VMEM_LIMIT = 56 * 1024 * 1024
FFN_UP_BN = 512

C_AB, C_AC, C_AH, C_CQ = 0, 1024, 2048, 3072
C_DQ, C_DK, C_DV = 4096, 5120, 6144
C_DU, C_DVV = 7168, 8192
C_CKV = 9216
C_KR = 9728


def _params(sem, vmem=VMEM_LIMIT):
    return pltpu.CompilerParams(dimension_semantics=sem, vmem_limit_bytes=vmem)


def _pick(n, target, mult):
    best = None
    for d in range(mult, min(n, target) + 1, mult):
        if n % d == 0:
            best = d
    assert best is not None, (n, target, mult)
    return best


def _mm_kernel(x_ref, w_ref, o_ref):
    o_ref[...] = jnp.dot(x_ref[...], w_ref[...], preferred_element_type=F32).astype(o_ref.dtype)


def _mm(x, w, bm, bn, out_dtype, name):
    m, k = x.shape
    n = w.shape[1]
    return pl.pallas_call(
        _mm_kernel,
        grid=(m // bm, n // bn),
        in_specs=[pl.BlockSpec((bm, k), lambda i, j: (i, 0)),
                  pl.BlockSpec((k, bn), lambda i, j: (0, j))],
        out_specs=pl.BlockSpec((bm, bn), lambda i, j: (i, j)),
        out_shape=jax.ShapeDtypeStruct((m, n), out_dtype),
        compiler_params=_params(("parallel", "arbitrary")),
        name=name,
    )(x, w)


def _conv_prompt_kernel(b_ref, c_ref, h_ref, cp_ref, hp_ref, w_ref, y_ref, st_ref, zz_ref):
    i = pl.program_id(0)
    bm = b_ref.shape[0]
    z = c_ref[...] * h_ref[...]
    zprev = cp_ref[...] * hp_ref[...]
    zz_ref[0:8, :] = jnp.where(i == 0, 0.0, zprev)
    zz_ref[8:, :] = z
    w = w_ref[...]
    y = w[0:1] * zz_ref[pl.ds(6, bm), :] + w[1:2] * zz_ref[pl.ds(7, bm), :] + w[2:3] * z
    y_ref[...] = (b_ref[...] * y).astype(y_ref.dtype)
    st_ref[...] = z[bm - 8:, :]


def _conv_prompt(h, conv_w, tp, bm):
    nb = tp // bm
    r8 = bm // 8
    cw = GROUP_W
    y, st = pl.pallas_call(
        _conv_prompt_kernel,
        grid=(nb,),
        in_specs=[pl.BlockSpec((bm, cw), lambda i: (i, C_AB // cw)),
                  pl.BlockSpec((bm, cw), lambda i: (i, C_AC // cw)),
                  pl.BlockSpec((bm, cw), lambda i: (i, C_AH // cw)),
                  pl.BlockSpec((8, cw), lambda i: (jnp.maximum(i * r8 - 1, 0), C_AC // cw)),
                  pl.BlockSpec((8, cw), lambda i: (jnp.maximum(i * r8 - 1, 0), C_AH // cw)),
                  pl.BlockSpec((CONV_W, cw), lambda i: (0, 0))],
        out_specs=[pl.BlockSpec((bm, cw), lambda i: (i, 0)),
                   pl.BlockSpec((8, cw), lambda i: (0, 0))],
        out_shape=[jax.ShapeDtypeStruct((tp, cw), BF16),
                   jax.ShapeDtypeStruct((8, cw), F32)],
        scratch_shapes=[pltpu.VMEM((bm + 8, cw), F32)],
        compiler_params=_params(("arbitrary",)),
        name="conv_prompt",
    )(h, h, h, h, h, conv_w)
    return y, st[8 - (CONV_W - 1):]


def _conv_sample_kernel(b_ref, c_ref, h_ref, past_ref, w_ref, y_ref, st_ref, zz_ref):
    ts = b_ref.shape[0]
    z = c_ref[...] * h_ref[...]
    zz_ref[pl.ds(6, 2), :] = past_ref[0]
    zz_ref[pl.ds(8, ts), :] = z
    w = w_ref[...]
    y = w[0:1] * zz_ref[pl.ds(6, ts), :] + w[1:2] * zz_ref[pl.ds(7, ts), :] + w[2:3] * z
    y_ref[...] = (b_ref[...] * y).astype(y_ref.dtype)
    st_ref[0] = zz_ref[pl.ds(ts + 6, 2), :]


def _conv_sample(h, past, conv_w, tp, nb, ts):
    cw = GROUP_W
    r0 = tp // ts
    return pl.pallas_call(
        _conv_sample_kernel,
        grid=(nb,),
        in_specs=[pl.BlockSpec((ts, cw), lambda b: (r0 + b, C_AB // cw)),
                  pl.BlockSpec((ts, cw), lambda b: (r0 + b, C_AC // cw)),
                  pl.BlockSpec((ts, cw), lambda b: (r0 + b, C_AH // cw)),
                  pl.BlockSpec((1, CONV_W - 1, cw), lambda b: (b, 0, 0)),
                  pl.BlockSpec((CONV_W, cw), lambda b: (0, 0))],
        out_specs=[pl.BlockSpec((ts, cw), lambda b: (b, 0)),
                   pl.BlockSpec((1, CONV_W - 1, cw), lambda b: (b, 0, 0))],
        out_shape=[jax.ShapeDtypeStruct((nb * ts, cw), BF16),
                   jax.ShapeDtypeStruct((nb, CONV_W - 1, cw), F32)],
        scratch_shapes=[pltpu.VMEM((ts + 8, cw), F32)],
        compiler_params=_params(("parallel",)),
        name="conv_sample",
    )(h, h, h, past, conv_w)


def _rope_spread(r, cos, sin):
    return r * cos + pltpu.roll(r, 64, 1) * sin


def _mla_q_kernel(x_ref, g_ref, w_ref, cos_ref, sin_ref, o_ref, *, scale):
    x = x_ref[...]
    cn = x * lax.rsqrt(jnp.mean(x * x, axis=-1, keepdims=True) + 1e-6) * g_ref[...]
    q = jnp.dot(cn.astype(BF16), w_ref[...], preferred_element_type=F32)
    cos = cos_ref[...]
    sin = sin_ref[...]
    for hd in range(HEADS):
        c0 = hd * 2 * LANES
        o_ref[:, c0:c0 + LANES] = (q[:, c0:c0 + LANES] * scale).astype(o_ref.dtype)
        r = _rope_spread(q[:, c0 + LANES:c0 + 2 * LANES], cos, sin)
        o_ref[:, c0 + LANES:c0 + 2 * LANES] = (r * scale).astype(o_ref.dtype)


def _mla_q(h, g, w, cos, sin, bm):
    t = h.shape[0]
    n = w.shape[1]
    scale = (MLA_NOPE + MLA_ROPE) ** -0.5 * LOG2E
    return pl.pallas_call(
        functools.partial(_mla_q_kernel, scale=scale),
        grid=(t // bm,),
        in_specs=[pl.BlockSpec((bm, GROUP_W), lambda i: (i, C_CQ // GROUP_W)),
                  pl.BlockSpec((1, GROUP_W), lambda i: (0, 0)),
                  pl.BlockSpec(w.shape, lambda i: (0, 0)),
                  pl.BlockSpec((bm, LANES), lambda i: (i, 0)),
                  pl.BlockSpec((bm, LANES), lambda i: (i, 0))],
        out_specs=pl.BlockSpec((bm, n), lambda i: (i, 0)),
        out_shape=jax.ShapeDtypeStruct((t, n), BF16),
        compiler_params=_params(("parallel",)),
        name="mla_q",
    )(h, g, w, cos, sin)


def _mla_kv_kernel(x_ref, kr_ref, g_ref, w_ref, cos_ref, sin_ref, ckv_ref, krf_ref, kv_ref, krb_ref):
    x = x_ref[...]
    cn = x * lax.rsqrt(jnp.mean(x * x, axis=-1, keepdims=True) + 1e-6) * g_ref[...]
    ckv_ref[...] = cn
    kv_ref[...] = jnp.dot(cn.astype(BF16), w_ref[...], preferred_element_type=F32).astype(kv_ref.dtype)
    r = _rope_spread(kr_ref[...], cos_ref[...], sin_ref[...])
    krf_ref[...] = r
    krb_ref[...] = r.astype(krb_ref.dtype)


def _mla_kv(h, g, w, cos, sin, bm):
    t = h.shape[0]
    n = w.shape[1]
    return pl.pallas_call(
        _mla_kv_kernel,
        grid=(t // bm,),
        in_specs=[pl.BlockSpec((bm, MLA_KV_RANK), lambda i: (i, C_CKV // MLA_KV_RANK)),
                  pl.BlockSpec((bm, LANES), lambda i: (i, C_KR // LANES)),
                  pl.BlockSpec((1, MLA_KV_RANK), lambda i: (0, 0)),
                  pl.BlockSpec(w.shape, lambda i: (0, 0)),
                  pl.BlockSpec((bm, LANES), lambda i: (i, 0)),
                  pl.BlockSpec((bm, LANES), lambda i: (i, 0))],
        out_specs=[pl.BlockSpec((bm, MLA_KV_RANK), lambda i: (i, 0)),
                   pl.BlockSpec((bm, LANES), lambda i: (i, 0)),
                   pl.BlockSpec((bm, n), lambda i: (i, 0)),
                   pl.BlockSpec((bm, LANES), lambda i: (i, 0))],
        out_shape=[jax.ShapeDtypeStruct((t, MLA_KV_RANK), F32),
                   jax.ShapeDtypeStruct((t, LANES), F32),
                   jax.ShapeDtypeStruct((t, n), BF16),
                   jax.ShapeDtypeStruct((t, LANES), BF16)],
        compiler_params=_params(("parallel",)),
        name="mla_kv",
    )(h, h, g, w, cos, sin)


def _nt_dot(a, b):
    return lax.dot_general(a, b, (((1,), (1,)), ((), ())), preferred_element_type=F32)


def _chunk_mask(qpos, kpos):
    return (kpos >> CHUNK_SHIFT) <= (qpos >> CHUNK_SHIFT)


def _tri_pairs(tp, bq, bk):
    qi, kj, last = [], [], []
    for i in range(tp // bq):
        q_hi = (i + 1) * bq - 1
        k_end = ((q_hi >> CHUNK_SHIFT) + 1) << CHUNK_SHIFT
        nj = -(-min(k_end, tp) // bk)
        for j in range(nj):
            qi.append(i)
            kj.append(j)
            last.append(1 if j == nj - 1 else 0)
    return (jnp.asarray(qi, jnp.int32), jnp.asarray(kj, jnp.int32), jnp.asarray(last, jnp.int32))


def _softmax_step_t(s_t, v_t, m_ref, l_ref, acc_ref, i, r0):
    m_old = m_ref[i]
    m_new = jnp.maximum(m_old, jnp.max(s_t, axis=0, keepdims=True))
    alpha = jnp.exp2(m_old - m_new)
    e = jnp.exp2(s_t - m_new)
    l_ref[i] = alpha * l_ref[i] + jnp.sum(e, axis=0, keepdims=True)
    d = v_t.shape[0]
    acc_ref[r0:r0 + d, :] = (alpha * acc_ref[r0:r0 + d, :]
                             + jnp.dot(v_t, e.astype(BF16), preferred_element_type=F32))
    m_ref[i] = m_new


def _flash_mla_kernel(qi_ref, kj_ref, last_ref, q_ref, kn_ref, kr_ref, vt_ref, o_ref, m_ref, l_ref, acc_ref):
    p = pl.program_id(0)
    bq = q_ref.shape[0]
    bk = kn_ref.shape[0]
    j = kj_ref[p]

    @pl.when(j == 0)
    def _():
        m_ref[...] = jnp.full(m_ref.shape, NEG_INF, F32)
        l_ref[...] = jnp.zeros(l_ref.shape, F32)
        acc_ref[...] = jnp.zeros(acc_ref.shape, F32)

    kpos = j * bk + lax.broadcasted_iota(jnp.int32, (bk, 1), 0)
    qpos = qi_ref[p] * bq + lax.broadcasted_iota(jnp.int32, (1, bq), 1)
    bias = jnp.where(_chunk_mask(qpos, kpos), 0.0, NEG_INF)
    kr = kr_ref[...]
    for hd in range(HEADS):
        c0 = hd * HEAD_W
        k = jnp.concatenate([kn_ref[:, c0:c0 + HEAD_W], kr], axis=1)
        s_t = _nt_dot(k, q_ref[:, 2 * c0:2 * c0 + 2 * LANES]) + bias
        _softmax_step_t(s_t, vt_ref[c0:c0 + HEAD_W, :], m_ref, l_ref, acc_ref, hd, c0)

    @pl.when(last_ref[p] == 1)
    def _():
        for hd in range(HEADS):
            c0 = hd * HEAD_W
            o_t = acc_ref[c0:c0 + HEAD_W, :] / l_ref[hd]
            o_ref[:, c0:c0 + HEAD_W] = jnp.transpose(o_t).astype(o_ref.dtype)


def _flash_mla(q, kv, krb, vt, tp, bq, bk):
    qi, kj, last = _tri_pairs(tp, bq, bk)
    hw = HEADS * HEAD_W
    grid_spec = pltpu.PrefetchScalarGridSpec(
        num_scalar_prefetch=3,
        grid=(qi.shape[0],),
        in_specs=[pl.BlockSpec((bq, 2 * hw), lambda p, qi, kj, la: (qi[p], 0)),
                  pl.BlockSpec((bk, hw), lambda p, qi, kj, la: (kj[p], 0)),
                  pl.BlockSpec((bk, LANES), lambda p, qi, kj, la: (kj[p], 0)),
                  pl.BlockSpec((hw, bk), lambda p, qi, kj, la: (0, kj[p]))],
        out_specs=pl.BlockSpec((bq, hw), lambda p, qi, kj, la: (qi[p], 0)),
        scratch_shapes=[pltpu.VMEM((HEADS, 1, bq), F32), pltpu.VMEM((HEADS, 1, bq), F32),
                        pltpu.VMEM((hw, bq), F32)],
    )
    return pl.pallas_call(
        _flash_mla_kernel,
        grid_spec=grid_spec,
        out_shape=jax.ShapeDtypeStruct((tp, hw), BF16),
        compiler_params=_params(("arbitrary",)),
        name="flash_mla",
    )(qi, kj, last, q, kv, krb, vt)


def _samp_mla_kernel(q_ref, kn_ref, kr_ref, v_ref, o_ref, *, past):
    ts = q_ref.shape[0]
    tk = kn_ref.shape[1]
    k = jnp.concatenate([kn_ref[0], kr_ref[0]], axis=1)
    s = _nt_dot(q_ref[...], k)
    qpos = past + lax.broadcasted_iota(jnp.int32, (ts, 1), 0)
    kpos = lax.broadcasted_iota(jnp.int32, (1, tk), 1)
    s = jnp.where(_chunk_mask(qpos, kpos) & (kpos < past + ts), s, NEG_INF)
    e = jnp.exp2(s - jnp.max(s, axis=1, keepdims=True))
    pv = jnp.dot(e.astype(BF16), v_ref[0], preferred_element_type=F32)
    o_ref[...] = (pv / jnp.sum(e, axis=1, keepdims=True)).astype(o_ref.dtype)


def _samp_mla(q, kvfull, krfull, tp, nb, ts, past):
    r0 = tp // ts
    tk = kvfull.shape[1]
    return pl.pallas_call(
        functools.partial(_samp_mla_kernel, past=past),
        grid=(nb, HEADS),
        in_specs=[pl.BlockSpec((ts, 2 * LANES), lambda b, h: (r0 + b, h)),
                  pl.BlockSpec((1, tk, LANES), lambda b, h: (b, 0, h)),
                  pl.BlockSpec((1, tk, LANES), lambda b, h: (b, 0, 0)),
                  pl.BlockSpec((1, tk, LANES), lambda b, h: (b, 0, HEADS + h))],
        out_specs=pl.BlockSpec((ts, LANES), lambda b, h: (b, h)),
        out_shape=jax.ShapeDtypeStruct((nb * ts, HEADS * LANES), BF16),
        compiler_params=_params(("parallel", "parallel")),
        name="samp_mla",
    )(q, kvfull, krfull, kvfull)


def _split_q(q, scale):
    lane = lax.broadcasted_iota(jnp.int32, q.shape, 1)
    qs = q * scale
    qa = jnp.where(lane < DIFF_DIM, qs, 0.0).astype(BF16)
    qb = jnp.where(lane >= DIFF_DIM, qs, 0.0).astype(BF16)
    return qa, qb


def _subln(o, g, lambda_init):
    o = o * lax.rsqrt(jnp.mean(o * o, axis=-1, keepdims=True) + 1e-5) * g
    return o * (1.0 - lambda_init)


def _flash_diff_kernel(qi_ref, kj_ref, last_ref, slope_ref, lam_ref, q_ref, k_ref, vt_ref, g_ref, o_ref,
                       qs_ref, m_ref, l_ref, acc_ref, *, lambda_init):
    p = pl.program_id(0)
    bq = q_ref.shape[0]
    bk = k_ref.shape[0]
    hw = HEADS * HEAD_W
    j = kj_ref[p]

    @pl.when(j == 0)
    def _():
        m_ref[...] = jnp.full(m_ref.shape, NEG_INF, F32)
        l_ref[...] = jnp.zeros(l_ref.shape, F32)
        acc_ref[...] = jnp.zeros(acc_ref.shape, F32)
        for hd in range(HEADS):
            c0 = hd * HEAD_W
            qa, qb = _split_q(q_ref[:, c0:c0 + HEAD_W], DIFF_DIM ** -0.5 * LOG2E)
            qs_ref[:, c0:c0 + HEAD_W] = qa
            qs_ref[:, hw + c0:hw + c0 + HEAD_W] = qb

    kpos = j * bk + lax.broadcasted_iota(jnp.int32, (bk, 1), 0)
    qpos = qi_ref[p] * bq + lax.broadcasted_iota(jnp.int32, (1, bq), 1)
    ndist = jnp.where(_chunk_mask(qpos, kpos), -LOG2E * jnp.abs(qpos - kpos).astype(F32), NEG_INF)
    for hd in range(HEADS):
        c0 = hd * HEAD_W
        k = k_ref[:, c0:c0 + HEAD_W]
        v_t = vt_ref[c0:c0 + HEAD_W, :]
        bias = slope_ref[hd] * ndist
        for mp in range(2):
            r0 = mp * hw + c0
            s_t = _nt_dot(k, qs_ref[:, r0:r0 + HEAD_W]) + bias
            _softmax_step_t(s_t, v_t, m_ref, l_ref, acc_ref, mp * HEADS + hd, r0)

    @pl.when(last_ref[p] == 1)
    def _():
        for hd in range(HEADS):
            c0 = hd * HEAD_W
            o_t = (acc_ref[c0:c0 + HEAD_W, :] / l_ref[hd]
                   - lam_ref[0] * (acc_ref[hw + c0:hw + c0 + HEAD_W, :] / l_ref[HEADS + hd]))
            o_t = o_t * lax.rsqrt(jnp.mean(o_t * o_t, axis=0, keepdims=True) + 1e-5)
            o = jnp.transpose(o_t) * g_ref[...] * (1.0 - lambda_init)
            o_ref[:, c0:c0 + HEAD_W] = o.astype(o_ref.dtype)


def _flash_diff(h, kb, vt, slopes, lam, g, lambda_init, tp, bq, bk):
    qi, kj, last = _tri_pairs(tp, bq, bk)
    hw = HEADS * HEAD_W
    smem = pl.BlockSpec(memory_space=pltpu.SMEM)
    grid_spec = pltpu.PrefetchScalarGridSpec(
        num_scalar_prefetch=3,
        grid=(qi.shape[0],),
        in_specs=[smem, smem,
                  pl.BlockSpec((bq, hw), lambda p, qi, kj, la: (qi[p], C_DQ // hw)),
                  pl.BlockSpec((bk, hw), lambda p, qi, kj, la: (kj[p], 0)),
                  pl.BlockSpec((hw, bk), lambda p, qi, kj, la: (0, kj[p])),
                  pl.BlockSpec((1, HEAD_W), lambda p, qi, kj, la: (0, 0))],
        out_specs=pl.BlockSpec((bq, hw), lambda p, qi, kj, la: (qi[p], 0)),
        scratch_shapes=[pltpu.VMEM((bq, 2 * hw), BF16),
                        pltpu.VMEM((2 * HEADS, 1, bq), F32), pltpu.VMEM((2 * HEADS, 1, bq), F32),
                        pltpu.VMEM((2 * hw, bq), F32)],
    )
    return pl.pallas_call(
        functools.partial(_flash_diff_kernel, lambda_init=lambda_init),
        grid_spec=grid_spec,
        out_shape=jax.ShapeDtypeStruct((tp, hw), BF16),
        compiler_params=_params(("arbitrary",)),
        name="flash_diff",
    )(qi, kj, last, slopes, lam, h, kb, vt, g)


def _samp_diff_kernel(slope_ref, lam_ref, q_ref, k_ref, v_ref, g_ref, o_ref, *, past, lambda_init):
    hd = pl.program_id(1)
    ts = q_ref.shape[0]
    tk = k_ref.shape[1]
    qa, qb = _split_q(q_ref[...], DIFF_DIM ** -0.5)
    k = k_ref[0]
    v = v_ref[0]
    qpos = past + lax.broadcasted_iota(jnp.int32, (ts, 1), 0)
    kpos = lax.broadcasted_iota(jnp.int32, (1, tk), 1)
    dist = jnp.abs(qpos - kpos).astype(F32)
    ok = _chunk_mask(qpos, kpos) & (kpos < past + ts)
    bias = jnp.where(ok, -slope_ref[hd] * dist, NEG_INF)
    outs = []
    for qx in (qa, qb):
        s = _nt_dot(qx, k) + bias
        e = jnp.exp(s - jnp.max(s, axis=1, keepdims=True))
        pv = jnp.dot(e.astype(BF16), v, preferred_element_type=F32)
        outs.append(pv / jnp.sum(e, axis=1, keepdims=True))
    o = outs[0] - lam_ref[0] * outs[1]
    o_ref[...] = _subln(o, g_ref[...], lambda_init).astype(o_ref.dtype)


def _samp_diff(h, kfull, vfull, slopes, lam, g, lambda_init, tp, nb, ts, past):
    r0 = tp // ts
    tk = kfull.shape[1]
    smem = pl.BlockSpec(memory_space=pltpu.SMEM)
    return pl.pallas_call(
        functools.partial(_samp_diff_kernel, past=past, lambda_init=lambda_init),
        grid=(nb, HEADS),
        in_specs=[smem, smem,
                  pl.BlockSpec((ts, HEAD_W), lambda b, h: (r0 + b, C_DQ // HEAD_W + h)),
                  pl.BlockSpec((1, tk, HEAD_W), lambda b, h: (b, 0, h)),
                  pl.BlockSpec((1, tk, HEAD_W), lambda b, h: (b, 0, h)),
                  pl.BlockSpec((1, HEAD_W), lambda b, h: (0, 0))],
        out_specs=pl.BlockSpec((ts, HEAD_W), lambda b, h: (b, h)),
        out_shape=jax.ShapeDtypeStruct((nb * ts, HEADS * HEAD_W), BF16),
        compiler_params=_params(("parallel", "parallel")),
        name="samp_diff",
    )(slopes, lam, h, kfull, vfull, g)


def _layernorm_rows(z, g, b):
    mu = jnp.mean(z, axis=-1, keepdims=True)
    var = jnp.mean(jnp.square(z - mu), axis=-1, keepdims=True)
    return (z - mu) * lax.rsqrt(var + 1e-5) * g + b


def _gelu(x):
    return 0.5 * x * (1.0 + jnp.tanh(math.sqrt(2.0 / math.pi) * (x + 0.044715 * (x * x * x))))


def _chunk_mlp_kernel(u_ref, v_ref, g_ref, b_ref, w_ref, bs_ref, y_ref, *mv_ref):
    u = _gelu(u_ref[...])
    v = _layernorm_rows(_gelu(v_ref[...]), g_ref[...], b_ref[...])
    if mv_ref:
        mv_ref[0][...] = v
    vb = v.astype(BF16)
    bs = bs_ref[...]
    for gi in range(GROUP_W // LANES):
        c0 = gi * LANES
        s = jnp.dot(w_ref[gi], vb[:, c0:c0 + LANES], preferred_element_type=F32) + bs[:, gi:gi + 1]
        y_ref[:, c0:c0 + LANES] = (u[:, c0:c0 + LANES] * s).astype(y_ref.dtype)


def _chunk_mlp(h, g, b, w_tril, bs_t, row0, nblk, ln, with_mv, name):
    r0 = row0 // ln
    cw = GROUP_W
    n_out = 2 if with_mv else 1
    return pl.pallas_call(
        _chunk_mlp_kernel,
        grid=(nblk,),
        in_specs=[pl.BlockSpec((ln, cw), lambda i: (r0 + i, C_DU // cw)),
                  pl.BlockSpec((ln, cw), lambda i: (r0 + i, C_DVV // cw)),
                  pl.BlockSpec((1, cw), lambda i: (0, 0)),
                  pl.BlockSpec((1, cw), lambda i: (0, 0)),
                  pl.BlockSpec(w_tril.shape, lambda i: (0, 0, 0)),
                  pl.BlockSpec(bs_t.shape, lambda i: (0, 0))],
        out_specs=[pl.BlockSpec((ln, cw), lambda i: (i, 0)),
                   pl.BlockSpec((ln, cw), lambda i: (i, 0))][:n_out],
        out_shape=[jax.ShapeDtypeStruct((nblk * ln, cw), BF16),
                   jax.ShapeDtypeStruct((nblk * ln, cw), F32)][:n_out],
        compiler_params=_params(("parallel",)),
        name=name,
    )(h, h, g, b, w_tril, bs_t)


def _mm4_kernel(a_ref, b_ref, c_ref, d_ref, w_ref, o_ref):
    gw = a_ref.shape[1]
    acc = jnp.dot(a_ref[...], w_ref[0:gw, :], preferred_element_type=F32)
    acc += jnp.dot(b_ref[...], w_ref[gw:2 * gw, :], preferred_element_type=F32)
    acc += jnp.dot(c_ref[...], w_ref[2 * gw:3 * gw, :], preferred_element_type=F32)
    acc += jnp.dot(d_ref[...], w_ref[3 * gw:4 * gw, :], preferred_element_type=F32)
    o_ref[...] = acc


def _out_proj(ya, yb, yc, yd, w, bm, bn):
    t, gw = ya.shape
    n = w.shape[1]
    xs = pl.BlockSpec((bm, gw), lambda i, j: (i, 0))
    return pl.pallas_call(
        _mm4_kernel,
        grid=(t // bm, n // bn),
        in_specs=[xs, xs, xs, xs, pl.BlockSpec((4 * gw, bn), lambda i, j: (0, j))],
        out_specs=pl.BlockSpec((bm, bn), lambda i, j: (i, j)),
        out_shape=jax.ShapeDtypeStruct((t, n), F32),
        compiler_params=_params(("parallel", "arbitrary")),
        name="out_proj",
    )(ya, yb, yc, yd, w)


def _ln_res_kernel(x_ref, y_ref, g_ref, b_ref, o_ref, ob_ref, *, alpha):
    out = _layernorm_rows(alpha * x_ref[...] + y_ref[...], g_ref[...], b_ref[...])
    o_ref[...] = out
    ob_ref[...] = out.astype(ob_ref.dtype)


def _ln_res(x, y, g, b, alpha, bm, name):
    t, d = x.shape
    return pl.pallas_call(
        functools.partial(_ln_res_kernel, alpha=alpha),
        grid=(t // bm,),
        in_specs=[pl.BlockSpec((bm, d), lambda i: (i, 0)),
                  pl.BlockSpec((bm, d), lambda i: (i, 0)),
                  pl.BlockSpec((1, d), lambda i: (0, 0)),
                  pl.BlockSpec((1, d), lambda i: (0, 0))],
        out_specs=[pl.BlockSpec((bm, d), lambda i: (i, 0)),
                   pl.BlockSpec((bm, d), lambda i: (i, 0))],
        out_shape=[jax.ShapeDtypeStruct((t, d), F32), jax.ShapeDtypeStruct((t, d), BF16)],
        compiler_params=_params(("parallel",)),
        name=name,
    )(x, y, g, b)


def _ffn_up_kernel(be_ref, nu_ref, x_ref, w1_ref, w3_ref, o_ref, w1b_ref, w3b_ref):
    i = pl.program_id(1)
    used = i < nu_ref[0]
    fresh = (i == 0) | (be_ref[i] != be_ref[jnp.maximum(i - 1, 0)])

    @pl.when(used & fresh)
    def _():
        w1b_ref[...] = w1_ref[0].astype(BF16)
        w3b_ref[...] = w3_ref[0].astype(BF16)

    @pl.when(used)
    def _():
        x = x_ref[...]
        a = jnp.dot(x, w1b_ref[...], preferred_element_type=F32)
        b = jnp.dot(x, w3b_ref[...], preferred_element_type=F32)
        o_ref[...] = (a * (1.0 / (1.0 + jnp.exp(-a))) * b).astype(o_ref.dtype)

    @pl.when(jnp.logical_not(used))
    def _():
        o_ref[...] = jnp.zeros(o_ref.shape, o_ref.dtype)


def _ffn_up(x, w1, w3, blk_e, n_used, bm, bn):
    m, k = x.shape
    n = w1.shape[2]

    def xmap(j, i, be, nu):
        return (jnp.minimum(i, nu[0] - 1), 0)

    def wmap(j, i, be, nu):
        return (be[jnp.minimum(i, nu[0] - 1)], 0, j)

    def omap(j, i, be, nu):
        return (i, j)

    grid_spec = pltpu.PrefetchScalarGridSpec(
        num_scalar_prefetch=2,
        grid=(n // bn, m // bm),
        in_specs=[pl.BlockSpec((bm, k), xmap),
                  pl.BlockSpec((1, k, bn), wmap),
                  pl.BlockSpec((1, k, bn), wmap)],
        out_specs=pl.BlockSpec((bm, bn), omap),
        scratch_shapes=[pltpu.VMEM((k, bn), BF16), pltpu.VMEM((k, bn), BF16)],
    )
    return pl.pallas_call(
        _ffn_up_kernel,
        grid_spec=grid_spec,
        out_shape=jax.ShapeDtypeStruct((m, n), BF16),
        compiler_params=_params(("parallel", "arbitrary")),
        name="ffn_up",
    )(blk_e, n_used, x, w1, w3)


def _ffn_down_kernel(be_ref, nu_ref, h_ref, w_ref, o_ref):
    kk = pl.program_id(1)
    part = jnp.dot(h_ref[...], w_ref[0], preferred_element_type=F32)

    @pl.when(kk == 0)
    def _():
        o_ref[...] = part

    @pl.when(kk > 0)
    def _():
        o_ref[...] += part


def _ffn_down_slab_kernel(be_ref, nu_ref, h_ref, w_ref, g_ref, o_ref, acc_ref):
    kk = pl.program_id(1)
    bm = acc_ref.shape[0]
    used = pl.program_id(0) < nu_ref[0]
    final = kk == pl.num_programs(1) - 1

    @pl.when(used)
    def _():
        part = jnp.dot(h_ref[...], w_ref[0], preferred_element_type=F32)

        @pl.when(kk == 0)
        def _():
            acc_ref[...] = part

        @pl.when(kk > 0)
        def _():
            acc_ref[...] += part

    @pl.when(used & final)
    def _():
        acc_ref[...] = acc_ref[...] * g_ref[...]

        def row(r, c):
            o_ref[r] = acc_ref[pl.ds(r, 1), :]
            return c

        lax.fori_loop(0, bm, row, 0, unroll=8)

    @pl.when(jnp.logical_not(used) & final)
    def _():
        o_ref[...] = jnp.zeros(o_ref.shape, o_ref.dtype)


def _ffn_down(hm, w2, gate, blk_e, n_used, bm, bk):
    m, kdim = hm.shape
    n = w2.shape[2]
    nk = kdim // bk

    def hmap(i, kk, be, nu):
        return (jnp.minimum(i, nu[0] - 1), jnp.where(i < nu[0], kk, nk - 1))

    def wmap(i, kk, be, nu):
        return (be[jnp.minimum(i, nu[0] - 1)], jnp.where(i < nu[0], kk, nk - 1), 0)

    def gmap(i, kk, be, nu):
        return (jnp.minimum(i, nu[0] - 1), 0)

    in_specs = [pl.BlockSpec((bm, bk), hmap), pl.BlockSpec((1, bk, n), wmap)]
    if gate is None:
        body, args, scratch = _ffn_down_kernel, (hm, w2), []
        out_spec = pl.BlockSpec((bm, n), lambda i, kk, be, nu: (i, 0))
        out_shape = jax.ShapeDtypeStruct((m, n), F32)
    else:
        body, args, scratch = _ffn_down_slab_kernel, (hm, w2, gate), [pltpu.VMEM((bm, n), F32)]
        in_specs.append(pl.BlockSpec((bm, 1), gmap))
        out_spec = pl.BlockSpec((bm, 1, n), lambda i, kk, be, nu: (i, 0, 0))
        out_shape = jax.ShapeDtypeStruct((m, 1, n), F32)
    grid_spec = pltpu.PrefetchScalarGridSpec(
        num_scalar_prefetch=2, grid=(m // bm, nk), in_specs=in_specs, out_specs=out_spec,
        scratch_shapes=scratch)
    return pl.pallas_call(
        body,
        grid_spec=grid_spec,
        out_shape=out_shape,
        compiler_params=_params(("arbitrary", "arbitrary")),
        name="ffn_down",
    )(blk_e, n_used, *args)


def _router_kernel(x_ref, w_ref, b_ref, idx_ref, gate_ref, cnt_ref, base_ref, *, n_experts):
    @pl.when(pl.program_id(0) == 0)
    def _():
        base_ref[...] = jnp.zeros(base_ref.shape, F32)

    bm = x_ref.shape[0]
    logits = jnp.dot(x_ref[...], w_ref[...], preferred_element_type=F32,
                     precision=lax.Precision.HIGHEST) + b_ref[...]
    lane = lax.broadcasted_iota(jnp.int32, logits.shape, 1)
    logits = jnp.where(lane < n_experts, logits, -jnp.inf)
    v1 = jnp.max(logits, axis=1, keepdims=True)
    i1 = jnp.min(jnp.where(logits == v1, lane, LANES), axis=1, keepdims=True)
    rest = jnp.where(lane == i1, -jnp.inf, logits)
    v2 = jnp.max(rest, axis=1, keepdims=True)
    i2 = jnp.min(jnp.where(rest == v2, lane, LANES), axis=1, keepdims=True)
    e2 = jnp.exp(v2 - v1)
    g1 = 1.0 / (1.0 + e2)
    g2 = e2 / (1.0 + e2)
    oh1 = (lane == i1).astype(F32)
    oh2 = (lane == i2).astype(F32)
    both = oh1 + oh2
    row = lax.broadcasted_iota(jnp.int32, (bm, bm), 0)
    col = lax.broadcasted_iota(jnp.int32, (bm, bm), 1)
    tril = (col <= row).astype(BF16)
    incl = jnp.dot(tril, both.astype(BF16), preferred_element_type=F32)
    before = base_ref[...] + incl - both
    r1 = jnp.sum(oh1 * before, axis=1, keepdims=True).astype(jnp.int32)
    r2 = jnp.sum(oh2 * before, axis=1, keepdims=True).astype(jnp.int32)
    total = base_ref[...] + incl[bm - 1:bm, :]
    base_ref[...] = total
    cnt_ref[...] = jnp.broadcast_to(total, cnt_ref.shape)
    idx_ref[...] = jnp.where(lane == 0, i1, jnp.where(lane == 1, i2,
                             jnp.where(lane == 2, r1, jnp.where(lane == 3, r2, 0))))
    gate_ref[...] = jnp.where(lane == 0, g1, jnp.where(lane == 1, g2, 0.0))


def _router(x, w_pad, b_pad, n_experts, bm):
    t, d = x.shape
    return pl.pallas_call(
        functools.partial(_router_kernel, n_experts=n_experts),
        grid=(t // bm,),
        in_specs=[pl.BlockSpec((bm, d), lambda i: (i, 0)),
                  pl.BlockSpec((d, LANES), lambda i: (0, 0)),
                  pl.BlockSpec((1, LANES), lambda i: (0, 0))],
        out_specs=[pl.BlockSpec((bm, LANES), lambda i: (i, 0)),
                   pl.BlockSpec((bm, LANES), lambda i: (i, 0)),
                   pl.BlockSpec((8, LANES), lambda i: (0, 0))],
        out_shape=[jax.ShapeDtypeStruct((t, LANES), jnp.int32), jax.ShapeDtypeStruct((t, LANES), F32),
                   jax.ShapeDtypeStruct((8, LANES), F32)],
        scratch_shapes=[pltpu.VMEM((1, LANES), F32)],
        compiler_params=_params(("arbitrary",)),
        name="router",
    )(x, w_pad, b_pad)


def _start_row_gathers(src_ref, idx_ref, first, buf_ref, sem, rows):
    def body(r, c):
        pltpu.make_async_copy(src_ref.at[idx_ref[first + r]], buf_ref.at[r], sem).start()
        return c

    lax.fori_loop(0, rows, body, 0, unroll=4)


def _wait_row_gathers(src_ref, buf_ref, sem, rows):
    def body(r, c):
        pltpu.make_async_copy(src_ref.at[0], buf_ref.at[r], sem).wait()
        return c

    lax.fori_loop(0, rows, body, 0, unroll=4)


def _gather_x_kernel(idx_ref, src_ref, o_ref, buf_ref, stage_ref, sem):
    i = pl.program_id(0)
    rows = stage_ref.shape[0]
    slot = i % 2

    @pl.when(i == 0)
    def _():
        _start_row_gathers(src_ref, idx_ref, 0, buf_ref.at[0], sem.at[0], rows)

    @pl.when(i + 1 < pl.num_programs(0))
    def _():
        _start_row_gathers(src_ref, idx_ref, (i + 1) * rows, buf_ref.at[1 - slot], sem.at[1 - slot], rows)

    _wait_row_gathers(src_ref, buf_ref.at[slot], sem.at[slot], rows)

    def row(r, c):
        stage_ref[pl.ds(r, 1), :] = buf_ref[slot, r]
        return c

    lax.fori_loop(0, rows, row, 0, unroll=8)
    o_ref[...] = stage_ref[...].astype(o_ref.dtype)


def _gather_x(src, idx, rows):
    n = idx.shape[0]
    width = src.shape[2]
    grid_spec = pltpu.PrefetchScalarGridSpec(
        num_scalar_prefetch=1,
        grid=(n // rows,),
        in_specs=[pl.BlockSpec(memory_space=pl.ANY)],
        out_specs=pl.BlockSpec((rows, width), lambda i, idx: (i, 0)),
        scratch_shapes=[pltpu.VMEM((2, rows, 1, width), F32), pltpu.VMEM((rows, width), F32),
                        pltpu.SemaphoreType.DMA((2,))],
    )
    return pl.pallas_call(
        _gather_x_kernel,
        grid_spec=grid_spec,
        out_shape=jax.ShapeDtypeStruct((n, width), BF16),
        compiler_params=_params(("arbitrary",)),
        name="gather_x",
    )(idx, src)


def _moe_combine_kernel(d0_ref, d1_ref, x_ref, y_ref, g_ref, b_ref, o_ref, ob_ref, buf_ref, stage_ref, sem,
                        *, alpha):
    i = pl.program_id(0)
    rows = stage_ref.shape[0]
    slot = i % 2

    def start(blk, s):
        _start_row_gathers(y_ref, d0_ref, blk * rows, buf_ref.at[s, 0], sem.at[s], rows)
        _start_row_gathers(y_ref, d1_ref, blk * rows, buf_ref.at[s, 1], sem.at[s], rows)

    @pl.when(i == 0)
    def _():
        start(0, 0)

    @pl.when(i + 1 < pl.num_programs(0))
    def _():
        start(i + 1, 1 - slot)

    _wait_row_gathers(y_ref, buf_ref.at[slot, 0], sem.at[slot], rows)
    _wait_row_gathers(y_ref, buf_ref.at[slot, 1], sem.at[slot], rows)

    def row(r, c):
        stage_ref[pl.ds(r, 1), :] = buf_ref[slot, 0, r] + buf_ref[slot, 1, r]
        return c

    lax.fori_loop(0, rows, row, 0, unroll=8)
    out = _layernorm_rows(alpha * x_ref[...] + stage_ref[...], g_ref[...], b_ref[...])
    o_ref[...] = out
    ob_ref[...] = out.astype(ob_ref.dtype)


def _moe_combine(x, y_slabs, d0, d1, g, b, alpha, rows):
    t, d = x.shape
    grid_spec = pltpu.PrefetchScalarGridSpec(
        num_scalar_prefetch=2,
        grid=(t // rows,),
        in_specs=[pl.BlockSpec((rows, d), lambda i, d0, d1: (i, 0)),
                  pl.BlockSpec(memory_space=pl.ANY),
                  pl.BlockSpec((1, d), lambda i, d0, d1: (0, 0)),
                  pl.BlockSpec((1, d), lambda i, d0, d1: (0, 0))],
        out_specs=[pl.BlockSpec((rows, d), lambda i, d0, d1: (i, 0)),
                   pl.BlockSpec((rows, d), lambda i, d0, d1: (i, 0))],
        scratch_shapes=[pltpu.VMEM((2, 2, rows, 1, d), F32), pltpu.VMEM((rows, d), F32),
                        pltpu.SemaphoreType.DMA((2,))],
    )
    return pl.pallas_call(
        functools.partial(_moe_combine_kernel, alpha=alpha),
        grid_spec=grid_spec,
        out_shape=[jax.ShapeDtypeStruct((t, d), F32), jax.ShapeDtypeStruct((t, d), BF16)],
        compiler_params=_params(("arbitrary",)),
        name="moe_combine",
    )(d0, d1, x, y_slabs, g, b)


def _moe(xf, w_router, b_router, w1, w3, w2, ln_g, ln_b, alpha, bm_e, bm_ln):
    t, d = xf.shape
    n_e = w1.shape[0]
    w_pad = jnp.zeros((d, LANES), F32).at[:, :n_e].set(w_router)
    b_pad = jnp.zeros((1, LANES), F32).at[0, :n_e].set(b_router)
    idx, gates, cnt = _router(xf, w_pad, b_pad, n_e, bm_ln)
    a = t * TOP_K
    flat_e = idx[:, :TOP_K].reshape(-1)
    rank = idx[:, TOP_K:2 * TOP_K].reshape(-1)
    flat_g = gates[:, :TOP_K].reshape(-1)
    flat_t = jnp.arange(a, dtype=jnp.int32) // TOP_K
    counts = cnt[0, :n_e].astype(jnp.int32)
    padded = (counts + bm_e - 1) // bm_e * bm_e
    pend = jnp.cumsum(padded)
    pstart = pend - padded
    dest = (jnp.take(pstart, flat_e) + rank).astype(jnp.int32)
    nb = -(-a // bm_e) + n_e
    tok = jnp.zeros(nb * bm_e, jnp.int32).at[dest].set(flat_t)
    gate = jnp.zeros(nb * bm_e, F32).at[dest].set(flat_g)
    blk_e = jnp.minimum(jnp.searchsorted(pend, jnp.arange(nb, dtype=jnp.int32) * bm_e, side='right'),
                        n_e - 1).astype(jnp.int32)
    n_used = (pend[-1] // bm_e).astype(jnp.int32).reshape(1)
    xg = _gather_x(xf.reshape(t, 1, d), tok, bm_e)
    hm = _ffn_up(xg, w1, w3, blk_e, n_used, bm_e, _pick(w1.shape[2], FFN_UP_BN, LANES))
    yb = _ffn_down(hm, w2, gate[:, None], blk_e, n_used, bm_e, _pick(w2.shape[1], 1024, LANES))
    dest2 = dest.reshape(t, TOP_K)
    return _moe_combine(xf, yb, dest2[:, 0], dest2[:, 1], ln_g, ln_b, alpha, bm_ln)


def _dense_ffn(xf, xb, w1, w3, w2, ln_g, ln_b, alpha, bm, bm_ln):
    t, d = xf.shape

    def one_expert(rows):
        return jnp.zeros((t // rows,), jnp.int32), jnp.full((1,), t // rows, jnp.int32)

    bm_up = _pick(t, 512, 16)
    hm = _ffn_up(xb, w1[None], w3[None], *one_expert(bm_up), bm_up, _pick(w1.shape[1], FFN_UP_BN, LANES))
    f = _ffn_down(hm, w2[None], None, *one_expert(bm), bm, _pick(w2.shape[0], 1024, LANES))
    return _ln_res(xf, f, ln_g, ln_b, alpha, bm_ln, "ln2_dense")


def _spread_rope_cols(w):
    z = jnp.zeros(w.shape[:-1] + (32,), w.dtype)
    return jnp.concatenate([w[..., :32], z, w[..., 32:], z], axis=-1)


def _prep_w_in(w):
    k0 = C_CQ + GROUP_W
    k1 = k0 + MLA_KV_RANK
    return jnp.concatenate([w[:, :k0], w[:, k1 + MLA_ROPE:], w[:, k0:k1],
                            _spread_rope_cols(w[:, k1:k1 + MLA_ROPE])], axis=1).astype(BF16)


def _prep_w_qb(w):
    r = w.shape[0]
    w = w.reshape(r, HEADS, MLA_NOPE + MLA_ROPE)
    out = jnp.concatenate([w[..., :MLA_NOPE], _spread_rope_cols(w[..., MLA_NOPE:])], axis=-1)
    return out.reshape(r, HEADS * 2 * LANES).astype(BF16)


def _prep_w_kvb(w):
    r = w.shape[0]
    w = w.reshape(r, HEADS, MLA_NOPE + HEAD_W)
    out = jnp.concatenate([w[..., :MLA_NOPE].reshape(r, -1), w[..., MLA_NOPE:].reshape(r, -1)], axis=-1)
    return out.astype(BF16)


def _rope_tables(pos):
    half = MLA_ROPE // 2
    freqs = ROPE_THETA ** (-jnp.arange(half, dtype=F32) / half)
    ang = pos.astype(F32)[:, None] * freqs
    cos, sin = jnp.cos(ang), jnp.sin(ang)
    z = jnp.zeros_like(cos)
    return (jnp.concatenate([cos, z, cos, z], axis=1), jnp.concatenate([-sin, z, sin, z], axis=1))


def _unspread(r):
    return jnp.concatenate([r[..., :32], r[..., 64:96]], axis=-1)


def _pad_keys(past, new, tk_pad):
    b, p, w = past.shape
    ts = new.shape[1]
    pad = jnp.zeros((b, tk_pad - p - ts, w), past.dtype)
    return jnp.concatenate([past, new, pad], axis=1)


def kernel(x_prompt, x_sample, cache_mla_ckv, cache_mla_krope, cache_diff_k, cache_diff_v, state_conv, w_in, conv_w, mla_q_norm, mla_w_qb, mla_kv_norm, mla_w_kvb, diff_lq1, diff_lk1, diff_lq2, diff_lk2, diff_subln, mlp_v_norm_g, mlp_v_norm_b, mlp_ws, mlp_bs, w_out, ln1_g, ln1_b, ln2_g, ln2_b, ffn_w1, ffn_w3, ffn_w2, moe_router, moe_router_b, moe_w1, moe_w3, moe_w2):
    nbp, tp, d = x_prompt.shape
    nbs, ts, _ = x_sample.shape
    depth = w_in.shape[0]
    past = cache_mla_ckv.shape[2]
    assert nbp == 1 and tp % MLP_CHUNK == 0 and ts % 16 == 0 and ts < MLP_CHUNK and tp % ts == 0
    t = tp + nbs * ts
    alpha = (2 * depth) ** 0.25
    tk_pad = -(-(past + ts) // LANES) * LANES

    bm_big = _pick(t, 1280, 16)
    bm_mid = _pick(t, 640, 16)
    bm_ln = _pick(t, 256, 16)
    bq = _pick(tp, 512, 128)
    bk = _pick(tp, 1024, 128)
    bm_conv = _pick(tp, 256, 8)

    pos = jnp.concatenate([jnp.arange(tp, dtype=jnp.int32),
                           jnp.tile(past + jnp.arange(ts, dtype=jnp.int32), nbs)])
    cos, sin = _rope_tables(pos)
    slopes = 2.0 ** (-8.0 * jnp.arange(1, HEADS + 1, dtype=F32) / HEADS)

    xf = jnp.concatenate([x_prompt.reshape(tp, d), x_sample.reshape(nbs * ts, d)], axis=0)
    xb = xf.astype(BF16)
    outs = [[] for _ in range(11)]
    for l in range(depth):
        lambda_init = 0.8 - 0.6 * math.exp(-0.3 * l)
        h = _mm(xb, _prep_w_in(w_in[l]), bm_big, 896, F32, "in_proj")

        ya_p, conv_p = _conv_prompt(h, conv_w[l], tp, bm_conv)
        ya_s, conv_s = _conv_sample(h, state_conv[l], conv_w[l], tp, nbs, ts)

        q = _mla_q(h, mla_q_norm[l][None], _prep_w_qb(mla_w_qb[l]), cos, sin, bm_mid)
        w_kvb = _prep_w_kvb(mla_w_kvb[l])
        ckv, krf, kv, krb = _mla_kv(h, mla_kv_norm[l][None], w_kvb, cos, sin, bm_mid)
        yb_p = _flash_mla(q, kv, krb, jnp.transpose(kv[:tp, HEADS * HEAD_W:]), tp, bq, bk)
        ckv_cache = cache_mla_ckv[l].reshape(nbs * past, MLA_KV_RANK).astype(BF16)
        kv_cache = _mm(ckv_cache, w_kvb, _pick(nbs * past, 1024, 16), w_kvb.shape[1], BF16, "kv_cache")
        kv_full = _pad_keys(kv_cache.reshape(nbs, past, -1), kv[tp:].reshape(nbs, ts, -1), tk_pad)
        kr_full = _pad_keys(_spread_rope_cols(cache_mla_krope[l]).astype(BF16),
                            krb[tp:].reshape(nbs, ts, LANES), tk_pad)
        yb_s = _samp_mla(q, kv_full, kr_full, tp, nbs, ts, past)

        lam = (jnp.exp(jnp.sum(diff_lq1[l] * diff_lk1[l])) - jnp.exp(jnp.sum(diff_lq2[l] * diff_lk2[l]))
               + lambda_init).reshape(1).astype(F32)
        sub_g = diff_subln[l][None]
        dk_new = h[:, C_DK:C_DK + GROUP_W]
        dv_new = h[:, C_DV:C_DV + GROUP_W]
        yc_p = _flash_diff(h, dk_new[:tp].astype(BF16), jnp.transpose(dv_new[:tp]).astype(BF16),
                           slopes, lam, sub_g, lambda_init, tp, bq, bk)
        dk_full = _pad_keys(cache_diff_k[l].reshape(nbs, past, GROUP_W).astype(BF16),
                            dk_new[tp:].reshape(nbs, ts, GROUP_W).astype(BF16), tk_pad)
        dv_full = _pad_keys(cache_diff_v[l].reshape(nbs, past, GROUP_W).astype(BF16),
                            dv_new[tp:].reshape(nbs, ts, GROUP_W).astype(BF16), tk_pad)
        yc_s = _samp_diff(h, dk_full, dv_full, slopes, lam, sub_g, lambda_init, tp, nbs, ts, past)

        tril = jnp.tril(jnp.ones((MLP_CHUNK, MLP_CHUNK), F32))
        w_sp = (mlp_ws[l] * tril).astype(BF16)
        mg, mb = mlp_v_norm_g[l][None], mlp_v_norm_b[l][None]
        yd_p, = _chunk_mlp(h, mg, mb, w_sp, jnp.transpose(mlp_bs[l]), 0, tp // MLP_CHUNK, MLP_CHUNK, False,
                           "chunk_mlp_prompt")
        yd_s, mv_s = _chunk_mlp(h, mg, mb, w_sp[:, :ts, :ts], jnp.transpose(mlp_bs[l][:, :ts]), tp, nbs, ts,
                                True, "chunk_mlp_sample")

        ya = jnp.concatenate([ya_p, ya_s], axis=0)
        yb = jnp.concatenate([yb_p, yb_s], axis=0)
        yc = jnp.concatenate([yc_p, yc_s], axis=0)
        yd = jnp.concatenate([yd_p, yd_s], axis=0)
        mix = _out_proj(ya, yb, yc, yd, w_out[l].astype(BF16), bm_big, 1024)
        xf, xb = _ln_res(xf, mix, ln1_g[l][None], ln1_b[l][None], alpha, bm_ln, "ln1")

        j = l // 2
        if l % 2 == 0:
            xf, xb = _dense_ffn(xf, xb, ffn_w1[j], ffn_w3[j], ffn_w2[j].astype(BF16),
                                ln2_g[l][None], ln2_b[l][None], alpha, bm_mid, bm_ln)
        else:
            xf, xb = _moe(xf, moe_router[j], moe_router_b[j], moe_w1[j], moe_w3[j], moe_w2[j].astype(BF16),
                          ln2_g[l][None], ln2_b[l][None], alpha, 512, bm_ln)

        kr_nat = _unspread(krf)
        new = (ckv[:tp].reshape(1, tp, -1), kr_nat[:tp].reshape(1, tp, -1),
               dk_new[:tp].reshape(1, tp, HEADS, HEAD_W), dv_new[:tp].reshape(1, tp, HEADS, HEAD_W),
               conv_p.reshape(1, CONV_W - 1, GROUP_W),
               ckv[tp:].reshape(nbs, ts, -1), kr_nat[tp:].reshape(nbs, ts, -1),
               dk_new[tp:].reshape(nbs, ts, HEADS, HEAD_W), dv_new[tp:].reshape(nbs, ts, HEADS, HEAD_W),
               conv_s, mv_s.reshape(nbs, ts, GROUP_W))
        for o, v in zip(outs, new):
            o.append(v)

    return (xf[:tp].reshape(1, tp, d), xf[tp:].reshape(nbs, ts, d)) + tuple(jnp.stack(o) for o in outs)
```
